```python
import math
import jax, jax.numpy as jnp
from jax import lax
import numpy as np

D_MODEL = 1024
BATCH = 4
SEQ = 4096
DEPTH = 2

N_MIXERS = 2
GRID_W = 64
Q_BLOCK = 128
EPS = 1e-6

A_HEAD_DIM = 64
A_N_HEADS = D_MODEL // (2 * A_HEAD_DIM)
A_ROT = A_HEAD_DIM // 4
ROPE_THETA_1D = 500000.0

B_HEAD_DIM = 128
B_N_HEADS = D_MODEL // B_HEAD_DIM
B_N_KV = max(1, B_N_HEADS // 4)
B_GROUP = B_N_HEADS // B_N_KV
ROPE_THETA_AXIAL = 10000.0

D_FF = -(-8 * D_MODEL // (3 * 256)) * 256

N_A_LAYERS = (DEPTH + 1) // 2
N_B_LAYERS = DEPTH // 2

kernel_name = "hybrid_diffattn_axialgqa_adaln_encoder"


def rmsnorm(x, g):
    xf = x.astype(jnp.float32)
    y = xf * lax.rsqrt(jnp.mean(xf * xf, axis=-1, keepdims=True) + EPS)
    return (y * g.astype(jnp.float32)).astype(x.dtype)


def rope_angles(pos, dim, theta):
    half = dim // 2
    freqs = theta ** (-jnp.arange(half, dtype=jnp.float32) / half)
    ang = pos.astype(jnp.float32)[:, None] * freqs[None, :]
    return jnp.cos(ang), jnp.sin(ang)


def apply_rope(x, cos, sin):
    shape = (1, x.shape[1]) + (1,) * (x.ndim - 3) + (cos.shape[-1],)
    c = cos.reshape(shape).astype(x.dtype)
    s = sin.reshape(shape).astype(x.dtype)
    x1, x2 = jnp.split(x, 2, axis=-1)
    return jnp.concatenate([x1 * c - x2 * s, x2 * c + x1 * s], axis=-1)


def lambda_init_fn(layer_idx):
    return 0.8 - 0.6 * math.exp(-0.3 * layer_idx)


def diff_attention(h, w_qkv, w_o, lq1, lk1, lq2, lk2, subln_g, lam_init, cos, sin):
    B_, S, _ = h.shape
    qkv = h @ w_qkv
    q, k, v = jnp.split(qkv, 3, axis=-1)
    q = q.reshape(B_, S, A_N_HEADS, 2, A_HEAD_DIM)
    k = k.reshape(B_, S, A_N_HEADS, 2, A_HEAD_DIM)
    v = v.reshape(B_, S, A_N_HEADS, 2 * A_HEAD_DIM)
    q = jnp.concatenate([apply_rope(q[..., :A_ROT], cos, sin), q[..., A_ROT:]], axis=-1)
    k = jnp.concatenate([apply_rope(k[..., :A_ROT], cos, sin), k[..., A_ROT:]], axis=-1)
    q = q * (A_HEAD_DIM ** -0.5)
    lam = (jnp.exp(jnp.sum(lq1.astype(jnp.float32) * lk1.astype(jnp.float32)))
           - jnp.exp(jnp.sum(lq2.astype(jnp.float32) * lk2.astype(jnp.float32)))
           + lam_init)
    nb = S // Q_BLOCK
    qb = q.reshape(B_, nb, Q_BLOCK, A_N_HEADS, 2, A_HEAD_DIM).transpose(1, 0, 2, 3, 4, 5)

    def block(qi):
        s = jnp.einsum('bqhcd,bkhcd->bhcqk', qi, k).astype(jnp.float32)
        p = jax.nn.softmax(s, axis=-1)
        w = (p[:, :, 0] - lam * p[:, :, 1]).astype(v.dtype)
        return jnp.einsum('bhqk,bkhe->bqhe', w, v)

    o = lax.map(block, qb)
    o = o.transpose(1, 0, 2, 3, 4).reshape(B_, S, A_N_HEADS, 2 * A_HEAD_DIM)
    o = rmsnorm(o, subln_g) * (1.0 - lam_init)
    return o.reshape(B_, S, A_N_HEADS * 2 * A_HEAD_DIM) @ w_o


def axial_gqa(h, w_qkv, w_o, qnorm_g, knorm_g, cos_r, sin_r, cos_c, sin_c):
    B_, S, _ = h.shape
    qkv = h @ w_qkv
    nq = B_N_HEADS * B_HEAD_DIM
    nkv = B_N_KV * B_HEAD_DIM
    q = qkv[..., :nq].reshape(B_, S, B_N_HEADS, B_HEAD_DIM)
    k = qkv[..., nq:nq + nkv].reshape(B_, S, B_N_KV, B_HEAD_DIM)
    v = qkv[..., nq + nkv:].reshape(B_, S, B_N_KV, B_HEAD_DIM)
    q = rmsnorm(q, qnorm_g)
    k = rmsnorm(k, knorm_g)
    half = B_HEAD_DIM // 2
    q = jnp.concatenate([apply_rope(q[..., :half], cos_r, sin_r), apply_rope(q[..., half:], cos_c, sin_c)], axis=-1)
    k = jnp.concatenate([apply_rope(k[..., :half], cos_r, sin_r), apply_rope(k[..., half:], cos_c, sin_c)], axis=-1)
    q = (q * (B_HEAD_DIM ** -0.5)).reshape(B_, S, B_N_KV, B_GROUP, B_HEAD_DIM)
    nb = S // Q_BLOCK
    qb = q.reshape(B_, nb, Q_BLOCK, B_N_KV, B_GROUP, B_HEAD_DIM).transpose(1, 0, 2, 3, 4, 5)

    def block(qi):
        s = jnp.einsum('bqhgd,bkhd->bhgqk', qi, k).astype(jnp.float32)
        p = jax.nn.softmax(s, axis=-1).astype(v.dtype)
        return jnp.einsum('bhgqk,bkhd->bqhgd', p, v)

    o = lax.map(block, qb)
    o = o.transpose(1, 0, 2, 3, 4, 5).reshape(B_, S, nq)
    return o @ w_o


def swiglu(h, w_in, w_out):
    gate, up = jnp.split(h @ w_in, 2, axis=-1)
    return (jax.nn.silu(gate) * up) @ w_out


def setup_inputs(seed: int = 0) -> dict:
    key = jax.random.key(seed)
    ks = jax.random.split(key, 24)
    f32 = jnp.float32
    D = D_MODEL
    nrm = lambda k, shape, s: jax.random.normal(k, shape, f32) * s
    b_qkv_out = B_N_HEADS * B_HEAD_DIM + 2 * B_N_KV * B_HEAD_DIM
    return {
        "x": nrm(ks[0], (BATCH, SEQ, D), 1.0),
        "c": nrm(ks[1], (BATCH, D), 1.0),
        "ada_w": nrm(ks[2], (DEPTH, D, 6 * D), 0.5 * D ** -0.5),
        "ada_b": nrm(ks[3], (DEPTH, 6 * D), 0.01),
        "norm1_g": 1.0 + nrm(ks[4], (DEPTH, D), 0.02),
        "norm2_g": 1.0 + nrm(ks[5], (DEPTH, D), 0.02),
        "a_w_qkv": nrm(ks[6], (N_A_LAYERS, D, 3 * D), D ** -0.5),
        "a_w_o": nrm(ks[7], (N_A_LAYERS, D, D), D ** -0.5),
        "a_lam_q1": nrm(ks[8], (N_A_LAYERS, A_HEAD_DIM), 0.1),
        "a_lam_k1": nrm(ks[9], (N_A_LAYERS, A_HEAD_DIM), 0.1),
        "a_lam_q2": nrm(ks[10], (N_A_LAYERS, A_HEAD_DIM), 0.1),
        "a_lam_k2": nrm(ks[11], (N_A_LAYERS, A_HEAD_DIM), 0.1),
        "a_subln_g": 1.0 + nrm(ks[12], (N_A_LAYERS, 2 * A_HEAD_DIM), 0.02),
        "b_w_qkv": nrm(ks[13], (N_B_LAYERS, D, b_qkv_out), D ** -0.5),
        "b_w_o": nrm(ks[14], (N_B_LAYERS, B_N_HEADS * B_HEAD_DIM, D), (B_N_HEADS * B_HEAD_DIM) ** -0.5),
        "b_qnorm_g": 1.0 + nrm(ks[15], (N_B_LAYERS, B_HEAD_DIM), 0.02),
        "b_knorm_g": 1.0 + nrm(ks[16], (N_B_LAYERS, B_HEAD_DIM), 0.02),
        "f_w_in": nrm(ks[17], (DEPTH, D, 2 * D_FF), D ** -0.5),
        "f_w_out": nrm(ks[18], (DEPTH, D_FF, D), D_FF ** -0.5),
        "final_g": 1.0 + nrm(ks[19], (D,), 0.02),
    }


def reference(x, c, ada_w, ada_b, norm1_g, norm2_g, a_w_qkv, a_w_o, a_lam_q1, a_lam_k1,
              a_lam_q2, a_lam_k2, a_subln_g, b_w_qkv, b_w_o, b_qnorm_g, b_knorm_g,
              f_w_in, f_w_out, final_g):
    S = x.shape[1]
    rows = S // GRID_W
    t = jnp.arange(S, dtype=jnp.int32)
    row_pos = jnp.broadcast_to(jnp.arange(rows, dtype=jnp.int32)[:, None], (rows, GRID_W)).reshape(S)
    col_pos = jnp.broadcast_to(jnp.arange(GRID_W, dtype=jnp.int32)[None, :], (rows, GRID_W)).reshape(S)
    cos_a, sin_a = rope_angles(t, A_ROT, ROPE_THETA_1D)
    cos_r, sin_r = rope_angles(row_pos, B_HEAD_DIM // 2, ROPE_THETA_AXIAL)
    cos_c, sin_c = rope_angles(col_pos, B_HEAD_DIM // 2, ROPE_THETA_AXIAL)
    cond = jax.nn.silu(c)

    for i in range(DEPTH):
        mod = (cond @ ada_w[i] + ada_b[i])[:, None, :]
        shift1, scale1, gate1, shift2, scale2, gate2 = jnp.split(mod, 6, axis=-1)
        h = rmsnorm(x, norm1_g[i]) * (1.0 + scale1) + shift1
        j = i // N_MIXERS
        if i % N_MIXERS == 0:
            y = diff_attention(h, a_w_qkv[j], a_w_o[j], a_lam_q1[j], a_lam_k1[j],
                               a_lam_q2[j], a_lam_k2[j], a_subln_g[j],
                               lambda_init_fn(i), cos_a, sin_a)
        else:
            y = axial_gqa(h, b_w_qkv[j], b_w_o[j], b_qnorm_g[j], b_knorm_g[j],
                          cos_r, sin_r, cos_c, sin_c)
        x = x + gate1 * y
        h = rmsnorm(x, norm2_g[i]) * (1.0 + scale2) + shift2
        x = x + gate2 * swiglu(h, f_w_in[i], f_w_out[i])

    return rmsnorm(x, final_g)
```

```python
import functools
import math

import jax
import jax.numpy as jnp
from jax.experimental import pallas as pl
from jax.experimental.pallas import tpu as pltpu

D_MODEL = 1024
DEPTH = 2
GRID_W = 64
EPS = 1e-6

A_HEAD_DIM = 64
A_N_HEADS = D_MODEL // (2 * A_HEAD_DIM)
A_ROT = A_HEAD_DIM // 4
ROPE_THETA_1D = 500000.0

B_HEAD_DIM = 128
B_N_HEADS = D_MODEL // B_HEAD_DIM
B_N_KV = max(1, B_N_HEADS // 4)
B_GROUP = B_N_HEADS // B_N_KV
ROPE_THETA_AXIAL = 10000.0

D_FF = -(-8 * D_MODEL // (3 * 256)) * 256

LANES = 128
SUBLANES = 8
VMEM_LIMIT_BYTES = 56 * 1024 * 1024

ROW_TILE = 512
Q_TILE_A = 256
Q_TILE_B = 128
KV_CHUNK = 512
FF_CHUNK = 256

BF16 = jnp.bfloat16
F32 = jnp.float32


def _lambda_init(layer_idx):
    return 0.8 - 0.6 * math.exp(-0.3 * layer_idx)


def _rms_scale(x):
    return jax.lax.rsqrt(jnp.mean(x * x, axis=-1, keepdims=True) + EPS)


def _mod_kernel(c_ref, w_ref, b_ref, o_ref):
    c = c_ref[...]
    cond = c * jax.nn.sigmoid(c)
    y = jnp.dot(cond.astype(BF16), w_ref[0].astype(BF16), preferred_element_type=F32)
    o_ref[0] = y + b_ref[0]


def _mod_call(c_pad, ada_w, ada_b):
    rows = c_pad.shape[0]
    n_chunks = 6
    return pl.pallas_call(
        _mod_kernel,
        grid=(DEPTH, n_chunks),
        in_specs=[
            pl.BlockSpec((rows, D_MODEL), lambda i, j: (0, 0)),
            pl.BlockSpec((1, D_MODEL, D_MODEL), lambda i, j: (i, 0, j)),
            pl.BlockSpec((1, 1, D_MODEL), lambda i, j: (i, 0, j)),
        ],
        out_specs=pl.BlockSpec((1, rows, D_MODEL), lambda i, j: (i, 0, j)),
        out_shape=jax.ShapeDtypeStruct((DEPTH, rows, 6 * D_MODEL), F32),
        compiler_params=pltpu.CompilerParams(
            dimension_semantics=("arbitrary", "arbitrary"),
            vmem_limit_bytes=VMEM_LIMIT_BYTES),
        name="adaln_mod",
    )(c_pad, ada_w, ada_b.reshape(DEPTH, 1, 6 * D_MODEL))


def _rope(x, cos_t, sin_fwd, sin_bwd, shift):
    fwd = pltpu.roll(x, LANES - shift, axis=1)
    bwd = pltpu.roll(x, shift, axis=1)
    return x * cos_t + fwd * sin_fwd + bwd * sin_bwd


def _modulated_norm(x, g, scale, shift):
    y = x * _rms_scale(x)
    return (y * g) * (1.0 + scale) + shift


def _qkv_a_kernel(x_ref, g_ref, shift_ref, scale_ref, w_ref, cos_ref, sf_ref, sb_ref, o_ref):
    h = _modulated_norm(x_ref[0], g_ref[...], scale_ref[0], shift_ref[0]).astype(BF16)
    cos_t, sf, sb = cos_ref[...], sf_ref[...], sb_ref[...]
    n_tiles = D_MODEL // LANES
    for t in range(3 * n_tiles):
        y = jnp.dot(h, w_ref[:, t * LANES:(t + 1) * LANES], preferred_element_type=F32)
        if t < 2 * n_tiles:
            y = _rope(y, cos_t, sf, sb, A_ROT // 2)
        if t < n_tiles:
            y = y * (A_HEAD_DIM ** -0.5)
        o_ref[0, :, t * LANES:(t + 1) * LANES] = y.astype(BF16)


def _qkv_b_kernel(x_ref, g_ref, shift_ref, scale_ref, w_ref, qg_ref, kg_ref,
                  cos_ref, sf_ref, sb_ref, o_ref):
    h = _modulated_norm(x_ref[0], g_ref[...], scale_ref[0], shift_ref[0]).astype(BF16)
    cos_t, sf, sb = cos_ref[...], sf_ref[...], sb_ref[...]
    n_q, n_k = B_N_HEADS, B_N_KV
    for t in range(n_q + 2 * n_k):
        y = jnp.dot(h, w_ref[:, t * LANES:(t + 1) * LANES], preferred_element_type=F32)
        if t < n_q + n_k:
            gain = qg_ref[...] if t < n_q else kg_ref[...]
            y = (y * _rms_scale(y)) * gain
            y = _rope(y, cos_t, sf, sb, B_HEAD_DIM // 4)
        if t < n_q:
            y = y * (B_HEAD_DIM ** -0.5)
        o_ref[0, :, t * LANES:(t + 1) * LANES] = y.astype(BF16)


def _mod_spec(layer, chunk):
    return pl.BlockSpec((1, 1, D_MODEL), lambda b, i: (layer * SUBLANES + b, 0, chunk))


def _const_spec(shape):
    return pl.BlockSpec(shape, lambda *_: (0,) * len(shape), pipeline_mode=pl.Buffered(1))


def _qkv_call(kernel_fn, name, layer, x, mod, norm_g, w, extra, tables):
    batch, seq, _ = x.shape
    n_out = w.shape[1]
    table_spec = pl.BlockSpec((ROW_TILE, LANES), lambda b, i: (i, 0))
    in_specs = [
        pl.BlockSpec((1, ROW_TILE, D_MODEL), lambda b, i: (b, i, 0)),
        _const_spec((1, D_MODEL)),
        _mod_spec(layer, 0),
        _mod_spec(layer, 1),
        _const_spec((D_MODEL, n_out)),
    ] + [_const_spec((1, LANES)) for _ in extra] + [table_spec] * 3
    return pl.pallas_call(
        kernel_fn,
        grid=(batch, seq // ROW_TILE),
        in_specs=in_specs,
        out_specs=pl.BlockSpec((1, ROW_TILE, n_out), lambda b, i: (b, i, 0)),
        out_shape=jax.ShapeDtypeStruct((batch, seq, n_out), BF16),
        compiler_params=pltpu.CompilerParams(
            dimension_semantics=("arbitrary", "arbitrary"),
            vmem_limit_bytes=VMEM_LIMIT_BYTES),
        name=name,
    )(x, norm_g.reshape(1, D_MODEL), mod, mod, w, *extra, *tables)


def _softmax_pv(q_rows, k_ref, vext_ref, acc_ref, seq):
    rows = q_rows.shape[0]
    m = jnp.full((rows, 1), -jnp.inf, F32)
    for j in range(seq // KV_CHUNK):
        k_j = k_ref[0, j * KV_CHUNK:(j + 1) * KV_CHUNK, :]
        s = jax.lax.dot_general(q_rows, k_j, (((1,), (1,)), ((), ())),
                                preferred_element_type=F32)
        m_new = jnp.maximum(m, jnp.max(s, axis=1, keepdims=True))
        p = jnp.exp(s - m_new).astype(BF16)
        pv = jnp.dot(p, vext_ref[j * KV_CHUNK:(j + 1) * KV_CHUNK, :], preferred_element_type=F32)
        if j == 0:
            acc_ref[...] = pv
        else:
            acc_ref[...] = jnp.exp(m - m_new) * acc_ref[...] + pv
        m = m_new


def _fill_vext(v_ref, vext_ref):
    vext_ref[:, :LANES] = v_ref[0]
    vext_ref[:, LANES:] = jnp.ones((v_ref.shape[1], LANES), BF16)


def _attn_a_kernel(lam_init, q_ref, k_ref, v_ref, lq1_ref, lk1_ref, lq2_ref, lk2_ref, g_ref,
                   o_ref, vext_ref, acc_ref):
    seq = k_ref.shape[1]
    tq = q_ref.shape[1]

    @pl.when(pl.program_id(2) == 0)
    def _():
        _fill_vext(v_ref, vext_ref)

    q = q_ref[0]
    lane = jax.lax.broadcasted_iota(jnp.int32, q.shape, 1)
    zero = jnp.zeros_like(q)
    q_rows = jnp.concatenate([jnp.where(lane < A_HEAD_DIM, q, zero),
                              jnp.where(lane >= A_HEAD_DIM, q, zero)], axis=0)
    _softmax_pv(q_rows, k_ref, vext_ref, acc_ref, seq)

    lam = (jnp.exp(jnp.sum(lq1_ref[...] * lk1_ref[...], axis=1, keepdims=True))
           - jnp.exp(jnp.sum(lq2_ref[...] * lk2_ref[...], axis=1, keepdims=True))
           + lam_init)
    o1 = acc_ref[:tq, :LANES] / acc_ref[:tq, LANES:]
    o2 = acc_ref[tq:, :LANES] / acc_ref[tq:, LANES:]
    o = o1 - lam * o2
    o = (o * _rms_scale(o)) * g_ref[...]
    o_ref[0] = (o * (1.0 - lam_init)).astype(BF16)


def _attn_b_kernel(q_ref, k_ref, v_ref, o_ref, vext_ref, acc_ref):
    seq = k_ref.shape[1]
    tq = q_ref.shape[1]

    @pl.when(pl.program_id(2) == 0)
    def _():
        _fill_vext(v_ref, vext_ref)

    q = q_ref[0]
    q_rows = jnp.concatenate([q[:, g * LANES:(g + 1) * LANES] for g in range(B_GROUP)], axis=0)
    _softmax_pv(q_rows, k_ref, vext_ref, acc_ref, seq)
    for g in range(B_GROUP):
        rows = slice(g * tq, (g + 1) * tq)
        o = acc_ref[rows, :LANES] / acc_ref[rows, LANES:]
        o_ref[0, :, g * LANES:(g + 1) * LANES] = o.astype(BF16)


def _attn_a_call(layer, qkv, lam_params, subln_g):
    batch, seq, _ = qkv.shape
    n_h = A_N_HEADS
    kernel_fn = functools.partial(_attn_a_kernel, _lambda_init(layer))
    small = [p.reshape(1, A_HEAD_DIM) for p in lam_params]
    return pl.pallas_call(
        kernel_fn,
        grid=(batch, n_h, seq // Q_TILE_A),
        in_specs=[
            pl.BlockSpec((1, Q_TILE_A, LANES), lambda b, h, i: (b, i, h)),
            pl.BlockSpec((1, seq, LANES), lambda b, h, i: (b, 0, n_h + h)),
            pl.BlockSpec((1, seq, LANES), lambda b, h, i: (b, 0, 2 * n_h + h)),
        ] + [_const_spec((1, A_HEAD_DIM))] * 4 + [_const_spec((1, LANES))],
        out_specs=pl.BlockSpec((1, Q_TILE_A, LANES), lambda b, h, i: (b, i, h)),
        out_shape=jax.ShapeDtypeStruct((batch, seq, D_MODEL), BF16),
        scratch_shapes=[pltpu.VMEM((seq, 2 * LANES), BF16),
                        pltpu.VMEM((2 * Q_TILE_A, 2 * LANES), F32)],
        compiler_params=pltpu.CompilerParams(
            dimension_semantics=("arbitrary", "arbitrary", "arbitrary"),
            vmem_limit_bytes=VMEM_LIMIT_BYTES),
        name="diff_attention",
    )(qkv, qkv, qkv, *small, subln_g.reshape(1, LANES))


def _attn_b_call(qkv):
    batch, seq, _ = qkv.shape
    group_w = B_GROUP * B_HEAD_DIM
    return pl.pallas_call(
        _attn_b_kernel,
        grid=(batch, B_N_KV, seq // Q_TILE_B),
        in_specs=[
            pl.BlockSpec((1, Q_TILE_B, group_w), lambda b, h, i: (b, i, h)),
            pl.BlockSpec((1, seq, LANES), lambda b, h, i: (b, 0, B_N_HEADS + h)),
            pl.BlockSpec((1, seq, LANES), lambda b, h, i: (b, 0, B_N_HEADS + B_N_KV + h)),
        ],
        out_specs=pl.BlockSpec((1, Q_TILE_B, group_w), lambda b, h, i: (b, i, h)),
        out_shape=jax.ShapeDtypeStruct((batch, seq, D_MODEL), BF16),
        scratch_shapes=[pltpu.VMEM((seq, 2 * LANES), BF16),
                        pltpu.VMEM((B_GROUP * Q_TILE_B, 2 * LANES), F32)],
        compiler_params=pltpu.CompilerParams(
            dimension_semantics=("arbitrary", "arbitrary", "arbitrary"),
            vmem_limit_bytes=VMEM_LIMIT_BYTES),
        name="axial_gqa_attention",
    )(qkv, qkv, qkv)


def _post_kernel(final_norm, x_ref, o_ref, wo_ref, gate1_ref, g2_ref, shift2_ref, scale2_ref,
                 gate2_ref, win_ref, wout_ref, fg_ref, out_ref, act_ref):
    y = jnp.dot(o_ref[0], wo_ref[...], preferred_element_type=F32)
    x1 = x_ref[0] + gate1_ref[0] * y
    out_ref[0] = x1
    h = _modulated_norm(x1, g2_ref[...], scale2_ref[0], shift2_ref[0]).astype(BF16)
    for c in range(D_FF // FF_CHUNK):
        cols = slice(c * FF_CHUNK, (c + 1) * FF_CHUNK)
        up_cols = slice(D_FF + c * FF_CHUNK, D_FF + (c + 1) * FF_CHUNK)
        gate = jnp.dot(h, win_ref[:, cols], preferred_element_type=F32)
        up = jnp.dot(h, win_ref[:, up_cols], preferred_element_type=F32)
        act_ref[:, cols] = ((gate * jax.nn.sigmoid(gate)) * up).astype(BF16)
    f = jnp.dot(act_ref[...], wout_ref[...], preferred_element_type=F32)
    x2 = out_ref[0] + gate2_ref[0] * f
    if final_norm:
        x2 = (x2 * _rms_scale(x2)) * fg_ref[...]
    out_ref[0] = x2


def _post_call(layer, final_norm, x, o, mod, w_o, norm2_g, w_in, w_out, final_g):
    batch, seq, _ = x.shape
    row_spec = pl.BlockSpec((1, ROW_TILE, D_MODEL), lambda b, i: (b, i, 0))
    return pl.pallas_call(
        functools.partial(_post_kernel, final_norm),
        grid=(batch, seq // ROW_TILE),
        in_specs=[
            row_spec,
            row_spec,
            _const_spec((D_MODEL, D_MODEL)),
            _mod_spec(layer, 2),
            _const_spec((1, D_MODEL)),
            _mod_spec(layer, 3),
            _mod_spec(layer, 4),
            _mod_spec(layer, 5),
            _const_spec((D_MODEL, 2 * D_FF)),
            _const_spec((D_FF, D_MODEL)),
            _const_spec((1, D_MODEL)),
        ],
        out_specs=row_spec,
        out_shape=jax.ShapeDtypeStruct((batch, seq, D_MODEL), F32),
        scratch_shapes=[pltpu.VMEM((ROW_TILE, D_FF), BF16)],
        compiler_params=pltpu.CompilerParams(
            dimension_semantics=("arbitrary", "arbitrary"),
            vmem_limit_bytes=VMEM_LIMIT_BYTES),
        name="wo_ffn",
    )(x, o, w_o, mod, norm2_g.reshape(1, D_MODEL), mod, mod, mod, w_in, w_out,
      final_g.reshape(1, D_MODEL))


def _rope_tables(pos_per_lane_group, half, theta, period, seq):
    freqs = theta ** (-jnp.arange(half, dtype=F32) / half)
    cos_cols, sf_cols, sb_cols = [], [], []
    for pos in pos_per_lane_group:
        ang = pos.astype(F32)[:, None] * freqs[None, :]
        c, s = jnp.cos(ang), jnp.sin(ang)
        z = jnp.zeros_like(s)
        cos_cols += [c, c]
        sf_cols += [-s, z]
        sb_cols += [z, s]
    rest = period - 2 * half * len(pos_per_lane_group)
    if rest:
        cos_cols.append(jnp.ones((seq, rest), F32))
        sf_cols.append(jnp.zeros((seq, rest), F32))
        sb_cols.append(jnp.zeros((seq, rest), F32))
    reps = LANES // period
    return tuple(jnp.tile(jnp.concatenate(cols, axis=1), (1, reps))
                 for cols in (cos_cols, sf_cols, sb_cols))


def kernel(x, c, ada_w, ada_b, norm1_g, norm2_g, a_w_qkv, a_w_o, a_lam_q1, a_lam_k1, a_lam_q2,
           a_lam_k2, a_subln_g, b_w_qkv, b_w_o, b_qnorm_g, b_knorm_g, f_w_in, f_w_out, final_g):
    batch, seq, _ = x.shape
    assert batch <= SUBLANES and seq % ROW_TILE == 0 and seq % KV_CHUNK == 0

    t = jnp.arange(seq, dtype=jnp.int32)
    tables_a = _rope_tables([t], A_ROT // 2, ROPE_THETA_1D, A_HEAD_DIM, seq)
    tables_b = _rope_tables([t // GRID_W, t % GRID_W], B_HEAD_DIM // 4, ROPE_THETA_AXIAL,
                            B_HEAD_DIM, seq)

    c_pad = jnp.zeros((SUBLANES, D_MODEL), F32).at[:batch].set(c)
    mod = _mod_call(c_pad, ada_w, ada_b).reshape(DEPTH * SUBLANES, 1, 6 * D_MODEL)

    for i in range(DEPTH):
        j = i // 2
        if i % 2 == 0:
            qkv = _qkv_call(_qkv_a_kernel, "norm_qkv_rope_a", i, x, mod, norm1_g[i],
                            a_w_qkv[j].astype(BF16), [], tables_a)
            o = _attn_a_call(i, qkv, (a_lam_q1[j], a_lam_k1[j], a_lam_q2[j], a_lam_k2[j]),
                             a_subln_g[j])
            w_o = a_w_o[j]
        else:
            qkv = _qkv_call(_qkv_b_kernel, "norm_qkv_rope_b", i, x, mod, norm1_g[i],
                            b_w_qkv[j].astype(BF16),
                            [b_qnorm_g[j].reshape(1, LANES), b_knorm_g[j].reshape(1, LANES)],
                            tables_b)
            o = _attn_b_call(qkv)
            w_o = b_w_o[j]
        x = _post_call(i, i == DEPTH - 1, x, o, mod, w_o.astype(BF16), norm2_g[i],
                       f_w_in[i].astype(BF16), f_w_out[i].astype(BF16), final_g)
    return x
```

```python
import functools
import math

import jax
import jax.numpy as jnp
from jax.experimental import pallas as pl
from jax.experimental.pallas import tpu as pltpu

D_MODEL = 1024
DEPTH = 2
GRID_W = 64
EPS = 1e-6
LOG2E = math.log2(math.e)

A_HEAD_DIM = 64
A_N_HEADS = D_MODEL // (2 * A_HEAD_DIM)
A_ROT = A_HEAD_DIM // 4
ROPE_THETA_1D = 500000.0

B_HEAD_DIM = 128
B_N_HEADS = D_MODEL // B_HEAD_DIM
B_N_KV = max(1, B_N_HEADS // 4)
B_GROUP = B_N_HEADS // B_N_KV
ROPE_THETA_AXIAL = 10000.0

D_FF = -(-8 * D_MODEL // (3 * 256)) * 256

LANES = 128
SUBLANES = 8
BF16_SUBLANES = 16
VMEM_LIMIT_BYTES = 56 * 1024 * 1024

ROW_TILE = 512
Q_LANES = 512
Q_TILE_A = Q_LANES // 2
Q_TILE_B = Q_LANES // B_GROUP
KV_CHUNK = 512
FF_CHUNK = 256
V_ROWS = LANES + BF16_SUBLANES

BF16 = jnp.bfloat16
F32 = jnp.float32


def _lambda_init(layer_idx):
    return 0.8 - 0.6 * math.exp(-0.3 * layer_idx)


def _rms_scale(x):
    return jax.lax.rsqrt(jnp.mean(x * x, axis=-1, keepdims=True) + EPS)


def _mod_kernel(c_ref, w_ref, b_ref, o_ref):
    c = c_ref[...]
    cond = c * jax.nn.sigmoid(c)
    y = jnp.dot(cond.astype(BF16), w_ref[0].astype(BF16), preferred_element_type=F32)
    o_ref[0] = y + b_ref[0]


def _mod_call(c_pad, ada_w, ada_b):
    rows = c_pad.shape[0]
    n_chunks = 6
    return pl.pallas_call(
        _mod_kernel,
        grid=(DEPTH, n_chunks),
        in_specs=[
            pl.BlockSpec((rows, D_MODEL), lambda i, j: (0, 0)),
            pl.BlockSpec((1, D_MODEL, D_MODEL), lambda i, j: (i, 0, j)),
            pl.BlockSpec((1, 1, D_MODEL), lambda i, j: (i, 0, j)),
        ],
        out_specs=pl.BlockSpec((1, rows, D_MODEL), lambda i, j: (i, 0, j)),
        out_shape=jax.ShapeDtypeStruct((DEPTH, rows, 6 * D_MODEL), F32),
        compiler_params=pltpu.CompilerParams(
            dimension_semantics=("arbitrary", "arbitrary"),
            vmem_limit_bytes=VMEM_LIMIT_BYTES),
        name="adaln_mod",
    )(c_pad, ada_w, ada_b.reshape(DEPTH, 1, 6 * D_MODEL))


def _rope(x, cos_t, sin_fwd, sin_bwd, shift):
    fwd = pltpu.roll(x, LANES - shift, axis=1)
    bwd = pltpu.roll(x, shift, axis=1)
    return x * cos_t + fwd * sin_fwd + bwd * sin_bwd


def _modulated_norm(x, g, scale, shift):
    y = x * _rms_scale(x)
    return (y * g) * (1.0 + scale) + shift


def _qkv_a_kernel(x_ref, g_ref, shift_ref, scale_ref, w_ref, cos_ref, sf_ref, sb_ref,
                  qt_ref, k_ref, vt_ref):
    h = _modulated_norm(x_ref[0], g_ref[...], scale_ref[0], shift_ref[0]).astype(BF16)
    cos_t, sf, sb = cos_ref[...], sf_ref[...], sb_ref[...]
    n_tiles = D_MODEL // LANES
    for t in range(3 * n_tiles):
        y = jnp.dot(h, w_ref[:, t * LANES:(t + 1) * LANES], preferred_element_type=F32)
        if t < 2 * n_tiles:
            y = _rope(y, cos_t, sf, sb, A_ROT // 2)
        if t < n_tiles:
            y = y * (A_HEAD_DIM ** -0.5 * LOG2E)
            qt_ref[0, t * LANES:(t + 1) * LANES, :] = y.T.astype(BF16)
        elif t < 2 * n_tiles:
            u = t - n_tiles
            k_ref[0, :, u * LANES:(u + 1) * LANES] = y.astype(BF16)
        else:
            u = t - 2 * n_tiles
            vt_ref[0, u * LANES:(u + 1) * LANES, :] = y.T.astype(BF16)


def _qkv_b_kernel(x_ref, g_ref, shift_ref, scale_ref, w_ref, qg_ref, kg_ref,
                  cos_ref, sf_ref, sb_ref, qt_ref, k_ref, vt_ref):
    h = _modulated_norm(x_ref[0], g_ref[...], scale_ref[0], shift_ref[0]).astype(BF16)
    cos_t, sf, sb = cos_ref[...], sf_ref[...], sb_ref[...]
    n_q, n_k = B_N_HEADS, B_N_KV
    for t in range(n_q + 2 * n_k):
        y = jnp.dot(h, w_ref[:, t * LANES:(t + 1) * LANES], preferred_element_type=F32)
        if t < n_q + n_k:
            gain = qg_ref[...] if t < n_q else kg_ref[...]
            y = (y * _rms_scale(y)) * gain
            y = _rope(y, cos_t, sf, sb, B_HEAD_DIM // 4)
        if t < n_q:
            y = y * (B_HEAD_DIM ** -0.5 * LOG2E)
            qt_ref[0, t * LANES:(t + 1) * LANES, :] = y.T.astype(BF16)
        elif t < n_q + n_k:
            u = t - n_q
            k_ref[0, :, u * LANES:(u + 1) * LANES] = y.astype(BF16)
        else:
            u = t - n_q - n_k
            vt_ref[0, u * LANES:(u + 1) * LANES, :] = y.T.astype(BF16)


def _mod_spec(layer, chunk):
    return pl.BlockSpec((1, 1, D_MODEL), lambda b, i: (layer * SUBLANES + b, 0, chunk))


def _const_spec(shape):
    return pl.BlockSpec(shape, lambda *_: (0,) * len(shape), pipeline_mode=pl.Buffered(1))


def _qkv_call(kernel_fn, name, layer, x, mod, norm_g, w, widths, extra, tables):
    batch, seq, _ = x.shape
    n_q, n_k, n_v = widths
    table_spec = pl.BlockSpec((ROW_TILE, LANES), lambda b, i: (i, 0))
    in_specs = [
        pl.BlockSpec((1, ROW_TILE, D_MODEL), lambda b, i: (b, i, 0)),
        _const_spec((1, D_MODEL)),
        _mod_spec(layer, 0),
        _mod_spec(layer, 1),
        _const_spec((D_MODEL, n_q + n_k + n_v)),
    ] + [_const_spec((1, LANES)) for _ in extra] + [table_spec] * 3
    return pl.pallas_call(
        kernel_fn,
        grid=(batch, seq // ROW_TILE),
        in_specs=in_specs,
        out_specs=[
            pl.BlockSpec((1, n_q, ROW_TILE), lambda b, i: (b, 0, i)),
            pl.BlockSpec((1, ROW_TILE, n_k), lambda b, i: (b, i, 0)),
            pl.BlockSpec((1, n_v, ROW_TILE), lambda b, i: (b, 0, i)),
        ],
        out_shape=[
            jax.ShapeDtypeStruct((batch, n_q, seq), BF16),
            jax.ShapeDtypeStruct((batch, seq, n_k), BF16),
            jax.ShapeDtypeStruct((batch, n_v, seq), BF16),
        ],
        compiler_params=pltpu.CompilerParams(
            dimension_semantics=("arbitrary", "arbitrary"),
            vmem_limit_bytes=VMEM_LIMIT_BYTES),
        name=name,
    )(x, norm_g.reshape(1, D_MODEL), mod, mod, w, *extra, *tables)


def _scores_and_weighted_sum(q_t, k_ref, vt_ref, write, read, seq):
    s_w_ref, m_w_ref = write
    s_r_ref, m_r_ref = read
    n = q_t.shape[1]
    m_prev = m_r_ref[0:1, :]
    col_max = None
    acc = None
    for j in range(seq // KV_CHUNK):
        rows = slice(j * KV_CHUNK, (j + 1) * KV_CHUNK)
        s = jnp.dot(k_ref[0, rows, :], q_t, preferred_element_type=F32)
        s_w_ref[rows, :] = s
        part = jnp.max(s.reshape(KV_CHUNK // SUBLANES, SUBLANES, n), axis=0)
        col_max = part if col_max is None else jnp.maximum(col_max, part)

        p = jnp.exp2(s_r_ref[rows, :] - m_prev).astype(BF16)
        pv = jnp.dot(vt_ref[:, rows], p, preferred_element_type=F32)
        acc = pv if acc is None else acc + pv
    m = jnp.max(col_max, axis=0, keepdims=True)
    m_w_ref[...] = jnp.broadcast_to(m, (SUBLANES, n))
    return acc


def _pipelined_step(tiles_per_head, v_ref, vt_ref, buffers, step_fn):
    t = pl.program_id(0)
    (s0_ref, m0_ref), (s1_ref, m1_ref) = buffers

    @pl.when(t == 0)
    def _():
        s1_ref[...] = jnp.zeros(s1_ref.shape, F32)
        m1_ref[...] = jnp.zeros(m1_ref.shape, F32)

    @pl.when(jnp.maximum(t - 1, 0) % tiles_per_head == 0)
    def _():
        vt_ref[:LANES, :] = v_ref[0]
        vt_ref[LANES:, :] = jnp.ones((V_ROWS - LANES, v_ref.shape[2]), BF16)

    @pl.when(t % 2 == 0)
    def _():
        step_fn(buffers[0], buffers[1])

    @pl.when(t % 2 == 1)
    def _():
        step_fn(buffers[1], buffers[0])


def _normalised(acc, cols):
    return acc[:LANES, cols] / acc[LANES:LANES + 1, cols]


def _attn_a_kernel(lam_init, tiles_per_head, q_ref, k_ref, v_ref, lq1_ref, lk1_ref, lq2_ref,
                   lk2_ref, g_ref, o_ref, vt_ref, s0_ref, m0_ref, s1_ref, m1_ref):
    seq = k_ref.shape[1]
    tq = q_ref.shape[2]

    def step(write, read):
        q = q_ref[0]
        row = jax.lax.broadcasted_iota(jnp.int32, q.shape, 0)
        zero = jnp.zeros_like(q)
        q_t = jnp.concatenate([jnp.where(row < A_HEAD_DIM, q, zero),
                               jnp.where(row >= A_HEAD_DIM, q, zero)], axis=1)
        acc = _scores_and_weighted_sum(q_t, k_ref, vt_ref, write, read, seq)
        lam = (jnp.exp(jnp.sum(lq1_ref[...] * lk1_ref[...], axis=1, keepdims=True))
               - jnp.exp(jnp.sum(lq2_ref[...] * lk2_ref[...], axis=1, keepdims=True))
               + lam_init)
        o_t = _normalised(acc, slice(0, tq)) - lam * _normalised(acc, slice(tq, 2 * tq))
        o = o_t.T
        o = (o * _rms_scale(o)) * g_ref[...]
        o_ref[0] = (o * (1.0 - lam_init)).astype(BF16)

    _pipelined_step(tiles_per_head, v_ref, vt_ref,
                    ((s0_ref, m0_ref), (s1_ref, m1_ref)), step)


def _attn_b_kernel(tiles_per_head, q_ref, k_ref, v_ref, o_ref, vt_ref,
                   s0_ref, m0_ref, s1_ref, m1_ref):
    seq = k_ref.shape[1]
    tq = q_ref.shape[2]

    def step(write, read):
        q = q_ref[0]
        q_t = jnp.concatenate([q[g * LANES:(g + 1) * LANES, :] for g in range(B_GROUP)], axis=1)
        acc = _scores_and_weighted_sum(q_t, k_ref, vt_ref, write, read, seq)
        for g in range(B_GROUP):
            o_t = _normalised(acc, slice(g * tq, (g + 1) * tq))
            o_ref[0, :, g * LANES:(g + 1) * LANES] = o_t.T.astype(BF16)

    _pipelined_step(tiles_per_head, v_ref, vt_ref,
                    ((s0_ref, m0_ref), (s1_ref, m1_ref)), step)


def _attn_scratch(seq):
    pair = [pltpu.VMEM((seq, Q_LANES), F32), pltpu.VMEM((SUBLANES, Q_LANES), F32)]
    return [pltpu.VMEM((V_ROWS, seq), BF16)] + pair + pair


def _tile_maps(n_tiles, n_heads, tiles_per_head):
    def split(tile):
        return (tile // (n_heads * tiles_per_head), (tile // tiles_per_head) % n_heads,
                tile % tiles_per_head)

    def current(t):
        return split(jnp.minimum(t, n_tiles - 1))

    def previous(t):
        return split(jnp.maximum(t - 1, 0))

    return current, previous


def _attn_specs(seq, q_block, o_block, current, previous):
    def q_map(t):
        b, h, i = current(t)
        return (b, h, i)

    def k_map(t):
        b, h, _ = current(t)
        return (b, 0, h)

    def v_map(t):
        b, h, _ = previous(t)
        return (b, h, 0)

    def o_map(t):
        b, h, i = previous(t)
        return (b, i, h)

    in_specs = [pl.BlockSpec(q_block, q_map),
                pl.BlockSpec((1, seq, LANES), k_map),
                pl.BlockSpec((1, LANES, seq), v_map)]
    return in_specs, pl.BlockSpec(o_block, o_map)


def _attn_a_call(layer, q_t, k, v_t, lam_params, subln_g):
    batch, seq, _ = k.shape
    tiles_per_head = seq // Q_TILE_A
    n_tiles = batch * A_N_HEADS * tiles_per_head
    current, previous = _tile_maps(n_tiles, A_N_HEADS, tiles_per_head)
    in_specs, out_spec = _attn_specs(seq, (1, LANES, Q_TILE_A), (1, Q_TILE_A, LANES),
                                     current, previous)
    kernel_fn = functools.partial(_attn_a_kernel, _lambda_init(layer), tiles_per_head)
    small = [p.reshape(1, A_HEAD_DIM) for p in lam_params]
    return pl.pallas_call(
        kernel_fn,
        grid=(n_tiles + 1,),
        in_specs=in_specs + [_const_spec((1, A_HEAD_DIM))] * 4 + [_const_spec((1, LANES))],
        out_specs=out_spec,
        out_shape=jax.ShapeDtypeStruct((batch, seq, D_MODEL), BF16),
        scratch_shapes=_attn_scratch(seq),
        compiler_params=pltpu.CompilerParams(
            dimension_semantics=("arbitrary",),
            vmem_limit_bytes=VMEM_LIMIT_BYTES),
        name="diff_attention",
    )(q_t, k, v_t, *small, subln_g.reshape(1, LANES))


def _attn_b_call(q_t, k, v_t):
    batch, seq, _ = k.shape
    group_w = B_GROUP * B_HEAD_DIM
    tiles_per_head = seq // Q_TILE_B
    n_tiles = batch * B_N_KV * tiles_per_head
    current, previous = _tile_maps(n_tiles, B_N_KV, tiles_per_head)
    in_specs, out_spec = _attn_specs(seq, (1, group_w, Q_TILE_B), (1, Q_TILE_B, group_w),
                                     current, previous)
    return pl.pallas_call(
        functools.partial(_attn_b_kernel, tiles_per_head),
        grid=(n_tiles + 1,),
        in_specs=in_specs,
        out_specs=out_spec,
        out_shape=jax.ShapeDtypeStruct((batch, seq, D_MODEL), BF16),
        scratch_shapes=_attn_scratch(seq),
        compiler_params=pltpu.CompilerParams(
            dimension_semantics=("arbitrary",),
            vmem_limit_bytes=VMEM_LIMIT_BYTES),
        name="axial_gqa_attention",
    )(q_t, k, v_t)


def _post_kernel(final_norm, x_ref, o_ref, wo_ref, gate1_ref, g2_ref, shift2_ref, scale2_ref,
                 gate2_ref, win_ref, wout_ref, fg_ref, out_ref, act_ref):
    y = jnp.dot(o_ref[0], wo_ref[...], preferred_element_type=F32)
    x1 = x_ref[0] + gate1_ref[0] * y
    out_ref[0] = x1
    h = _modulated_norm(x1, g2_ref[...], scale2_ref[0], shift2_ref[0]).astype(BF16)
    for c in range(D_FF // FF_CHUNK):
        cols = slice(c * FF_CHUNK, (c + 1) * FF_CHUNK)
        up_cols = slice(D_FF + c * FF_CHUNK, D_FF + (c + 1) * FF_CHUNK)
        gate = jnp.dot(h, win_ref[:, cols], preferred_element_type=F32)
        up = jnp.dot(h, win_ref[:, up_cols], preferred_element_type=F32)
        act_ref[:, cols] = ((gate * jax.nn.sigmoid(gate)) * up).astype(BF16)
    f = jnp.dot(act_ref[...], wout_ref[...], preferred_element_type=F32)
    x2 = out_ref[0] + gate2_ref[0] * f
    if final_norm:
        x2 = (x2 * _rms_scale(x2)) * fg_ref[...]
    out_ref[0] = x2


def _post_call(layer, final_norm, x, o, mod, w_o, norm2_g, w_in, w_out, final_g):
    batch, seq, _ = x.shape
    row_spec = pl.BlockSpec((1, ROW_TILE, D_MODEL), lambda b, i: (b, i, 0))
    return pl.pallas_call(
        functools.partial(_post_kernel, final_norm),
        grid=(batch, seq // ROW_TILE),
        in_specs=[
            row_spec,
            row_spec,
            _const_spec((D_MODEL, D_MODEL)),
            _mod_spec(layer, 2),
            _const_spec((1, D_MODEL)),
            _mod_spec(layer, 3),
            _mod_spec(layer, 4),
            _mod_spec(layer, 5),
            _const_spec((D_MODEL, 2 * D_FF)),
            _const_spec((D_FF, D_MODEL)),
            _const_spec((1, D_MODEL)),
        ],
        out_specs=row_spec,
        out_shape=jax.ShapeDtypeStruct((batch, seq, D_MODEL), F32),
        scratch_shapes=[pltpu.VMEM((ROW_TILE, D_FF), BF16)],
        compiler_params=pltpu.CompilerParams(
            dimension_semantics=("arbitrary", "arbitrary"),
            vmem_limit_bytes=VMEM_LIMIT_BYTES),
        name="wo_ffn",
    )(x, o, w_o, mod, norm2_g.reshape(1, D_MODEL), mod, mod, mod, w_in, w_out,
      final_g.reshape(1, D_MODEL))


def _rope_tables(pos_per_lane_group, half, theta, period, seq):
    freqs = theta ** (-jnp.arange(half, dtype=F32) / half)
    cos_cols, sf_cols, sb_cols = [], [], []
    for pos in pos_per_lane_group:
        ang = pos.astype(F32)[:, None] * freqs[None, :]
        c, s = jnp.cos(ang), jnp.sin(ang)
        z = jnp.zeros_like(s)
        cos_cols += [c, c]
        sf_cols += [-s, z]
        sb_cols += [z, s]
    rest = period - 2 * half * len(pos_per_lane_group)
    if rest:
        cos_cols.append(jnp.ones((seq, rest), F32))
        sf_cols.append(jnp.zeros((seq, rest), F32))
        sb_cols.append(jnp.zeros((seq, rest), F32))
    reps = LANES // period
    return tuple(jnp.tile(jnp.concatenate(cols, axis=1), (1, reps))
                 for cols in (cos_cols, sf_cols, sb_cols))


def kernel(x, c, ada_w, ada_b, norm1_g, norm2_g, a_w_qkv, a_w_o, a_lam_q1, a_lam_k1, a_lam_q2,
           a_lam_k2, a_subln_g, b_w_qkv, b_w_o, b_qnorm_g, b_knorm_g, f_w_in, f_w_out, final_g):
    batch, seq, _ = x.shape
    assert batch <= SUBLANES and seq % ROW_TILE == 0 and seq % KV_CHUNK == 0

    t = jnp.arange(seq, dtype=jnp.int32)
    tables_a = _rope_tables([t], A_ROT // 2, ROPE_THETA_1D, A_HEAD_DIM, seq)
    tables_b = _rope_tables([t // GRID_W, t % GRID_W], B_HEAD_DIM // 4, ROPE_THETA_AXIAL,
                            B_HEAD_DIM, seq)

    c_pad = jnp.zeros((SUBLANES, D_MODEL), F32).at[:batch].set(c)
    mod = _mod_call(c_pad, ada_w, ada_b).reshape(DEPTH * SUBLANES, 1, 6 * D_MODEL)

    for i in range(DEPTH):
        j = i // 2
        if i % 2 == 0:
            q_t, k, v_t = _qkv_call(_qkv_a_kernel, "norm_qkv_rope_a", i, x, mod, norm1_g[i],
                                    a_w_qkv[j].astype(BF16), (D_MODEL,) * 3, [], tables_a)
            o = _attn_a_call(i, q_t, k, v_t,
                             (a_lam_q1[j], a_lam_k1[j], a_lam_q2[j], a_lam_k2[j]), a_subln_g[j])
            w_o = a_w_o[j]
        else:
            kv_w = B_N_KV * B_HEAD_DIM
            q_t, k, v_t = _qkv_call(_qkv_b_kernel, "norm_qkv_rope_b", i, x, mod, norm1_g[i],
                                    b_w_qkv[j].astype(BF16), (D_MODEL, kv_w, kv_w),
                                    [b_qnorm_g[j].reshape(1, LANES),
                                     b_knorm_g[j].reshape(1, LANES)], tables_b)
            o = _attn_b_call(q_t, k, v_t)
            w_o = b_w_o[j]
        x = _post_call(i, i == DEPTH - 1, x, o, mod, w_o.astype(BF16), norm2_g[i],
                       f_w_in[i].astype(BF16), f_w_out[i].astype(BF16), final_g)
    return x
```

```python
import functools
import math

import jax
import jax.numpy as jnp
from jax.experimental import pallas as pl
from jax.experimental.pallas import tpu as pltpu

D_MODEL = 1024
DEPTH = 2
GRID_W = 64
EPS = 1e-6
LOG2E = math.log2(math.e)

A_HEAD_DIM = 64
A_N_HEADS = D_MODEL // (2 * A_HEAD_DIM)
A_ROT = A_HEAD_DIM // 4
ROPE_THETA_1D = 500000.0

B_HEAD_DIM = 128
B_N_HEADS = D_MODEL // B_HEAD_DIM
B_N_KV = max(1, B_N_HEADS // 4)
B_GROUP = B_N_HEADS // B_N_KV
ROPE_THETA_AXIAL = 10000.0

D_FF = -(-8 * D_MODEL // (3 * 256)) * 256

LANES = 128
SUBLANES = 8
BF16_SUBLANES = 16
VMEM_LIMIT_BYTES = 56 * 1024 * 1024

ROW_TILE = 512
PROJ_WIDTH = 512
Q_LANES = 1024
Q_TILE_A = Q_LANES // 2
Q_TILE_B = Q_LANES // B_GROUP
KV_CHUNK = 512
FF_CHUNK = 256
V_ROWS = LANES + BF16_SUBLANES

BF16 = jnp.bfloat16
F32 = jnp.float32


def _lambda_init(layer_idx):
    return 0.8 - 0.6 * math.exp(-0.3 * layer_idx)


def _rms_scale(x):
    return jax.lax.rsqrt(jnp.mean(x * x, axis=-1, keepdims=True) + EPS)


def _mod_kernel(c_ref, w_ref, b_ref, o_ref):
    c = c_ref[...]
    cond = c * jax.nn.sigmoid(c)
    y = jnp.dot(cond.astype(BF16), w_ref[0].astype(BF16), preferred_element_type=F32)
    o_ref[0] = y + b_ref[0]


def _mod_call(c_pad, ada_w, ada_b):
    rows = c_pad.shape[0]
    n_chunks = 6
    return pl.pallas_call(
        _mod_kernel,
        grid=(DEPTH, n_chunks),
        in_specs=[
            pl.BlockSpec((rows, D_MODEL), lambda i, j: (0, 0)),
            pl.BlockSpec((1, D_MODEL, D_MODEL), lambda i, j: (i, 0, j)),
            pl.BlockSpec((1, 1, D_MODEL), lambda i, j: (i, 0, j)),
        ],
        out_specs=pl.BlockSpec((1, rows, D_MODEL), lambda i, j: (i, 0, j)),
        out_shape=jax.ShapeDtypeStruct((DEPTH, rows, 6 * D_MODEL), F32),
        compiler_params=pltpu.CompilerParams(
            dimension_semantics=("arbitrary", "arbitrary"),
            vmem_limit_bytes=VMEM_LIMIT_BYTES),
        name="adaln_mod",
    )(c_pad, ada_w, ada_b.reshape(DEPTH, 1, 6 * D_MODEL))


def _rope(x, cos_t, sin_fwd, sin_bwd, shift):
    fwd = pltpu.roll(x, LANES - shift, axis=1)
    bwd = pltpu.roll(x, shift, axis=1)
    return x * cos_t + fwd * sin_fwd + bwd * sin_bwd


def _modulated_norm(x, g, scale, shift):
    y = x * _rms_scale(x)
    return (y * g) * (1.0 + scale) + shift


def _transposed_projection(w_t_ref, h, block):
    n_rows = w_t_ref.shape[0]
    h_t = h.T
    for r in range(0, n_rows, block):
        yield r, jnp.dot(w_t_ref[r:min(r + block, n_rows), :], h_t, preferred_element_type=F32)


def _qkv_a_kernel(x_ref, g_ref, shift_ref, scale_ref, wqv_t_ref, wk_ref, cos_ref, sf_ref, sb_ref,
                  cos_t_ref, sin_t_ref, qt_ref, k_ref, vt_ref):
    h = _modulated_norm(x_ref[0], g_ref[...], scale_ref[0], shift_ref[0]).astype(BF16)

    half = A_ROT // 2
    cos_t, sf, sb = cos_ref[...], sf_ref[...], sb_ref[...]
    for c in range(0, D_MODEL, PROJ_WIDTH):
        y = jnp.dot(h, wk_ref[:, c:c + PROJ_WIDTH], preferred_element_type=F32)
        for u in range(0, PROJ_WIDTH, LANES):
            k = _rope(y[:, u:u + LANES], cos_t, sf, sb, half)
            k_ref[0, :, c + u:c + u + LANES] = k.astype(BF16)

    cos_rows, sin_rows = cos_t_ref[...], sin_t_ref[...]
    for r, y in _transposed_projection(wqv_t_ref, h, PROJ_WIDTH):
        if r < D_MODEL:
            pieces = []
            for base in range(0, PROJ_WIDTH, A_HEAD_DIM):
                lo, hi = y[base:base + half], y[base + half:base + A_ROT]
                pieces += [lo * cos_rows - hi * sin_rows, hi * cos_rows + lo * sin_rows,
                           y[base + A_ROT:base + A_HEAD_DIM]]
            q_t = jnp.concatenate(pieces, axis=0) * (A_HEAD_DIM ** -0.5 * LOG2E)
            qt_ref[0, r:r + PROJ_WIDTH, :] = q_t.astype(BF16)
        else:
            vt_ref[0, r - D_MODEL:r - D_MODEL + PROJ_WIDTH, :] = y.astype(BF16)


def _qkv_b_kernel(x_ref, g_ref, shift_ref, scale_ref, wqv_t_ref, wk_ref, qg_ref, kg_ref,
                  cos_ref, sf_ref, sb_ref, cos_t_ref, sin_t_ref, qt_ref, k_ref, vt_ref):
    h = _modulated_norm(x_ref[0], g_ref[...], scale_ref[0], shift_ref[0]).astype(BF16)

    quarter = B_HEAD_DIM // 4
    cos_t, sf, sb = cos_ref[...], sf_ref[...], sb_ref[...]
    y = jnp.dot(h, wk_ref[...], preferred_element_type=F32)
    for u in range(0, wk_ref.shape[1], LANES):
        k = y[:, u:u + LANES]
        k = (k * _rms_scale(k)) * kg_ref[...]
        k = _rope(k, cos_t, sf, sb, quarter)
        k_ref[0, :, u:u + LANES] = k.astype(BF16)

    cos_rows, sin_rows, q_gain = cos_t_ref[...], sin_t_ref[...], qg_ref[...]
    for r, y in _transposed_projection(wqv_t_ref, h, PROJ_WIDTH):
        if r < D_MODEL:
            for base in range(0, PROJ_WIDTH, B_HEAD_DIM):
                q = y[base:base + B_HEAD_DIM]
                inv = jax.lax.rsqrt(jnp.mean(q * q, axis=0, keepdims=True) + EPS)
                q = (q * inv) * q_gain
                partner = jnp.concatenate(
                    [q[quarter:2 * quarter], q[:quarter], q[3 * quarter:], q[2 * quarter:3 * quarter]],
                    axis=0)
                q = (q * cos_rows + partner * sin_rows) * (B_HEAD_DIM ** -0.5 * LOG2E)
                qt_ref[0, r + base:r + base + B_HEAD_DIM, :] = q.astype(BF16)
        else:
            vt_ref[0, r - D_MODEL:r - D_MODEL + y.shape[0], :] = y.astype(BF16)


def _mod_spec(layer, chunk):
    return pl.BlockSpec((1, 1, D_MODEL), lambda b, i: (layer * SUBLANES + b, 0, chunk))


def _const_spec(shape):
    return pl.BlockSpec(shape, lambda *_: (0,) * len(shape), pipeline_mode=pl.Buffered(1))


def _qkv_call(kernel_fn, name, layer, x, mod, norm_g, w_qv_t, w_k, extra, tables, tables_t):
    batch, seq, _ = x.shape
    n_q = D_MODEL
    n_v = w_qv_t.shape[0] - n_q
    n_k = w_k.shape[1]
    table_spec = pl.BlockSpec((ROW_TILE, LANES), lambda b, i: (i, 0))
    table_t_spec = pl.BlockSpec((tables_t[0].shape[0], ROW_TILE), lambda b, i: (0, i))
    in_specs = [
        pl.BlockSpec((1, ROW_TILE, D_MODEL), lambda b, i: (b, i, 0)),
        _const_spec((1, D_MODEL)),
        _mod_spec(layer, 0),
        _mod_spec(layer, 1),
        _const_spec(w_qv_t.shape),
        _const_spec(w_k.shape),
    ] + [_const_spec(e.shape) for e in extra] + [table_spec] * 3 + [table_t_spec] * 2
    return pl.pallas_call(
        kernel_fn,
        grid=(batch, seq // ROW_TILE),
        in_specs=in_specs,
        out_specs=[
            pl.BlockSpec((1, n_q, ROW_TILE), lambda b, i: (b, 0, i)),
            pl.BlockSpec((1, ROW_TILE, n_k), lambda b, i: (b, i, 0)),
            pl.BlockSpec((1, n_v, ROW_TILE), lambda b, i: (b, 0, i)),
        ],
        out_shape=[
            jax.ShapeDtypeStruct((batch, n_q, seq), BF16),
            jax.ShapeDtypeStruct((batch, seq, n_k), BF16),
            jax.ShapeDtypeStruct((batch, n_v, seq), BF16),
        ],
        compiler_params=pltpu.CompilerParams(
            dimension_semantics=("arbitrary", "arbitrary"),
            vmem_limit_bytes=VMEM_LIMIT_BYTES),
        name=name,
    )(x, norm_g.reshape(1, D_MODEL), mod, mod, w_qv_t, w_k, *extra, *tables, *tables_t)


def _scores_and_weighted_sum(q_t, k_ref, vt_ref, write, read, seq):
    s_w_ref, m_w_ref = write
    s_r_ref, m_r_ref = read
    n = q_t.shape[1]
    m_prev = m_r_ref[0:1, :]
    col_max = None
    acc = None
    for j in range(seq // KV_CHUNK):
        rows = slice(j * KV_CHUNK, (j + 1) * KV_CHUNK)
        s = jnp.dot(k_ref[0, rows, :], q_t, preferred_element_type=F32)
        s_w_ref[rows, :] = s
        part = jnp.max(s.reshape(KV_CHUNK // SUBLANES, SUBLANES, n), axis=0)
        col_max = part if col_max is None else jnp.maximum(col_max, part)

        p = jnp.exp2(s_r_ref[rows, :] - m_prev).astype(BF16)
        pv = jnp.dot(vt_ref[:, rows], p, preferred_element_type=F32)
        acc = pv if acc is None else acc + pv
    m = jnp.max(col_max, axis=0, keepdims=True)
    m_w_ref[...] = jnp.broadcast_to(m, (SUBLANES, n))
    return acc


def _pipelined_step(tiles_per_head, v_ref, vt_ref, buffers, step_fn):
    t = pl.program_id(0)
    (s0_ref, m0_ref), (s1_ref, m1_ref) = buffers

    @pl.when(t == 0)
    def _():
        s1_ref[...] = jnp.zeros(s1_ref.shape, F32)
        m1_ref[...] = jnp.zeros(m1_ref.shape, F32)

    @pl.when(jnp.maximum(t - 1, 0) % tiles_per_head == 0)
    def _():
        vt_ref[:LANES, :] = v_ref[0]
        vt_ref[LANES:, :] = jnp.ones((V_ROWS - LANES, v_ref.shape[2]), BF16)

    @pl.when(t % 2 == 0)
    def _():
        step_fn(buffers[0], buffers[1])

    @pl.when(t % 2 == 1)
    def _():
        step_fn(buffers[1], buffers[0])


def _normalised(acc, cols):
    return acc[:LANES, cols] / acc[LANES:LANES + 1, cols]


def _attn_a_kernel(lam_init, tiles_per_head, q_ref, k_ref, v_ref, lq1_ref, lk1_ref, lq2_ref,
                   lk2_ref, g_ref, o_ref, vt_ref, s0_ref, m0_ref, s1_ref, m1_ref):
    seq = k_ref.shape[1]
    tq = q_ref.shape[2]

    def step(write, read):
        q = q_ref[0]
        row = jax.lax.broadcasted_iota(jnp.int32, q.shape, 0)
        zero = jnp.zeros_like(q)
        q_t = jnp.concatenate([jnp.where(row < A_HEAD_DIM, q, zero),
                               jnp.where(row >= A_HEAD_DIM, q, zero)], axis=1)
        acc = _scores_and_weighted_sum(q_t, k_ref, vt_ref, write, read, seq)
        lam = (jnp.exp(jnp.sum(lq1_ref[...] * lk1_ref[...], axis=1, keepdims=True))
               - jnp.exp(jnp.sum(lq2_ref[...] * lk2_ref[...], axis=1, keepdims=True))
               + lam_init)
        o_t = _normalised(acc, slice(0, tq)) - lam * _normalised(acc, slice(tq, 2 * tq))
        o = o_t.T
        o = (o * _rms_scale(o)) * g_ref[...]
        o_ref[0] = (o * (1.0 - lam_init)).astype(BF16)

    _pipelined_step(tiles_per_head, v_ref, vt_ref,
                    ((s0_ref, m0_ref), (s1_ref, m1_ref)), step)


def _attn_b_kernel(tiles_per_head, q_ref, k_ref, v_ref, o_ref, vt_ref,
                   s0_ref, m0_ref, s1_ref, m1_ref):
    seq = k_ref.shape[1]
    tq = q_ref.shape[2]

    def step(write, read):
        q = q_ref[0]
        q_t = jnp.concatenate([q[g * LANES:(g + 1) * LANES, :] for g in range(B_GROUP)], axis=1)
        acc = _scores_and_weighted_sum(q_t, k_ref, vt_ref, write, read, seq)
        for g in range(B_GROUP):
            o_t = _normalised(acc, slice(g * tq, (g + 1) * tq))
            o_ref[0, :, g * LANES:(g + 1) * LANES] = o_t.T.astype(BF16)

    _pipelined_step(tiles_per_head, v_ref, vt_ref,
                    ((s0_ref, m0_ref), (s1_ref, m1_ref)), step)


def _attn_scratch(seq):
    pair = [pltpu.VMEM((seq, Q_LANES), F32), pltpu.VMEM((SUBLANES, Q_LANES), F32)]
    return [pltpu.VMEM((V_ROWS, seq), BF16)] + pair + pair


def _tile_maps(n_tiles, n_heads, tiles_per_head):
    def split(tile):
        return (tile // (n_heads * tiles_per_head), (tile // tiles_per_head) % n_heads,
                tile % tiles_per_head)

    def current(t):
        return split(jnp.minimum(t, n_tiles - 1))

    def previous(t):
        return split(jnp.maximum(t - 1, 0))

    return current, previous


def _attn_specs(seq, q_block, o_block, current, previous):
    def q_map(t):
        b, h, i = current(t)
        return (b, h, i)

    def k_map(t):
        b, h, _ = current(t)
        return (b, 0, h)

    def v_map(t):
        b, h, _ = previous(t)
        return (b, h, 0)

    def o_map(t):
        b, h, i = previous(t)
        return (b, i, h)

    in_specs = [pl.BlockSpec(q_block, q_map),
                pl.BlockSpec((1, seq, LANES), k_map),
                pl.BlockSpec((1, LANES, seq), v_map)]
    return in_specs, pl.BlockSpec(o_block, o_map)


def _attn_a_call(layer, q_t, k, v_t, lam_params, subln_g):
    batch, seq, _ = k.shape
    tiles_per_head = seq // Q_TILE_A
    n_tiles = batch * A_N_HEADS * tiles_per_head
    current, previous = _tile_maps(n_tiles, A_N_HEADS, tiles_per_head)
    in_specs, out_spec = _attn_specs(seq, (1, LANES, Q_TILE_A), (1, Q_TILE_A, LANES),
                                     current, previous)
    kernel_fn = functools.partial(_attn_a_kernel, _lambda_init(layer), tiles_per_head)
    small = [p.reshape(1, A_HEAD_DIM) for p in lam_params]
    return pl.pallas_call(
        kernel_fn,
        grid=(n_tiles + 1,),
        in_specs=in_specs + [_const_spec((1, A_HEAD_DIM))] * 4 + [_const_spec((1, LANES))],
        out_specs=out_spec,
        out_shape=jax.ShapeDtypeStruct((batch, seq, D_MODEL), BF16),
        scratch_shapes=_attn_scratch(seq),
        compiler_params=pltpu.CompilerParams(
            dimension_semantics=("arbitrary",),
            vmem_limit_bytes=VMEM_LIMIT_BYTES),
        name="diff_attention",
    )(q_t, k, v_t, *small, subln_g.reshape(1, LANES))


def _attn_b_call(q_t, k, v_t):
    batch, seq, _ = k.shape
    group_w = B_GROUP * B_HEAD_DIM
    tiles_per_head = seq // Q_TILE_B
    n_tiles = batch * B_N_KV * tiles_per_head
    current, previous = _tile_maps(n_tiles, B_N_KV, tiles_per_head)
    in_specs, out_spec = _attn_specs(seq, (1, group_w, Q_TILE_B), (1, Q_TILE_B, group_w),
                                     current, previous)
    return pl.pallas_call(
        functools.partial(_attn_b_kernel, tiles_per_head),
        grid=(n_tiles + 1,),
        in_specs=in_specs,
        out_specs=out_spec,
        out_shape=jax.ShapeDtypeStruct((batch, seq, D_MODEL), BF16),
        scratch_shapes=_attn_scratch(seq),
        compiler_params=pltpu.CompilerParams(
            dimension_semantics=("arbitrary",),
            vmem_limit_bytes=VMEM_LIMIT_BYTES),
        name="axial_gqa_attention",
    )(q_t, k, v_t)


def _post_kernel(final_norm, x_ref, o_ref, wo_ref, gate1_ref, g2_ref, shift2_ref, scale2_ref,
                 gate2_ref, win_ref, wout_ref, fg_ref, out_ref, act_ref):
    y = jnp.dot(o_ref[0], wo_ref[...], preferred_element_type=F32)
    x1 = x_ref[0] + gate1_ref[0] * y
    out_ref[0] = x1
    h = _modulated_norm(x1, g2_ref[...], scale2_ref[0], shift2_ref[0]).astype(BF16)
    for c in range(D_FF // FF_CHUNK):
        cols = slice(c * FF_CHUNK, (c + 1) * FF_CHUNK)
        up_cols = slice(D_FF + c * FF_CHUNK, D_FF + (c + 1) * FF_CHUNK)
        gate = jnp.dot(h, win_ref[:, cols], preferred_element_type=F32)
        up = jnp.dot(h, win_ref[:, up_cols], preferred_element_type=F32)
        act_ref[:, cols] = ((gate * jax.nn.sigmoid(gate)) * up).astype(BF16)
    f = jnp.dot(act_ref[...], wout_ref[...], preferred_element_type=F32)
    x2 = out_ref[0] + gate2_ref[0] * f
    if final_norm:
        x2 = (x2 * _rms_scale(x2)) * fg_ref[...]
    out_ref[0] = x2


def _post_call(layer, final_norm, x, o, mod, w_o, norm2_g, w_in, w_out, final_g):
    batch, seq, _ = x.shape
    row_spec = pl.BlockSpec((1, ROW_TILE, D_MODEL), lambda b, i: (b, i, 0))
    return pl.pallas_call(
        functools.partial(_post_kernel, final_norm),
        grid=(batch, seq // ROW_TILE),
        in_specs=[
            row_spec,
            row_spec,
            _const_spec((D_MODEL, D_MODEL)),
            _mod_spec(layer, 2),
            _const_spec((1, D_MODEL)),
            _mod_spec(layer, 3),
            _mod_spec(layer, 4),
            _mod_spec(layer, 5),
            _const_spec((D_MODEL, 2 * D_FF)),
            _const_spec((D_FF, D_MODEL)),
            _const_spec((1, D_MODEL)),
        ],
        out_specs=row_spec,
        out_shape=jax.ShapeDtypeStruct((batch, seq, D_MODEL), F32),
        scratch_shapes=[pltpu.VMEM((ROW_TILE, D_FF), BF16)],
        compiler_params=pltpu.CompilerParams(
            dimension_semantics=("arbitrary", "arbitrary"),
            vmem_limit_bytes=VMEM_LIMIT_BYTES),
        name="wo_ffn",
    )(x, o, w_o, mod, norm2_g.reshape(1, D_MODEL), mod, mod, mod, w_in, w_out,
      final_g.reshape(1, D_MODEL))


def _rope_angles(pos, half, theta):
    freqs = theta ** (-jnp.arange(half, dtype=F32) / half)
    ang = pos.astype(F32)[:, None] * freqs[None, :]
    return jnp.cos(ang), jnp.sin(ang)


def _rope_tables(pos_per_lane_group, half, theta, period, seq):
    cos_cols, sf_cols, sb_cols = [], [], []
    for pos in pos_per_lane_group:
        c, s = _rope_angles(pos, half, theta)
        z = jnp.zeros_like(s)
        cos_cols += [c, c]
        sf_cols += [-s, z]
        sb_cols += [z, s]
    rest = period - 2 * half * len(pos_per_lane_group)
    if rest:
        cos_cols.append(jnp.ones((seq, rest), F32))
        sf_cols.append(jnp.zeros((seq, rest), F32))
        sb_cols.append(jnp.zeros((seq, rest), F32))
    reps = LANES // period
    return tuple(jnp.tile(jnp.concatenate(cols, axis=1), (1, reps))
                 for cols in (cos_cols, sf_cols, sb_cols))


def kernel(x, c, ada_w, ada_b, norm1_g, norm2_g, a_w_qkv, a_w_o, a_lam_q1, a_lam_k1, a_lam_q2,
           a_lam_k2, a_subln_g, b_w_qkv, b_w_o, b_qnorm_g, b_knorm_g, f_w_in, f_w_out, final_g):
    batch, seq, _ = x.shape
    assert batch <= SUBLANES and seq % ROW_TILE == 0 and seq % KV_CHUNK == 0

    t = jnp.arange(seq, dtype=jnp.int32)
    row_pos, col_pos = t // GRID_W, t % GRID_W
    tables_a = _rope_tables([t], A_ROT // 2, ROPE_THETA_1D, A_HEAD_DIM, seq)
    tables_b = _rope_tables([row_pos, col_pos], B_HEAD_DIM // 4, ROPE_THETA_AXIAL,
                            B_HEAD_DIM, seq)
    cos_a, sin_a = _rope_angles(t, A_ROT // 2, ROPE_THETA_1D)
    tables_a_t = (cos_a.T, sin_a.T)
    cos_r, sin_r = _rope_angles(row_pos, B_HEAD_DIM // 4, ROPE_THETA_AXIAL)
    cos_c, sin_c = _rope_angles(col_pos, B_HEAD_DIM // 4, ROPE_THETA_AXIAL)
    tables_b_t = (jnp.concatenate([cos_r, cos_r, cos_c, cos_c], axis=1).T,
                  jnp.concatenate([-sin_r, sin_r, -sin_c, sin_c], axis=1).T)

    c_pad = jnp.zeros((SUBLANES, D_MODEL), F32).at[:batch].set(c)
    mod = _mod_call(c_pad, ada_w, ada_b).reshape(DEPTH * SUBLANES, 1, 6 * D_MODEL)

    for i in range(DEPTH):
        j = i // 2
        if i % 2 == 0:
            w = a_w_qkv[j].astype(BF16)
            w_qv_t = jnp.concatenate([w[:, :D_MODEL], w[:, 2 * D_MODEL:]], axis=1).T
            q_t, k, v_t = _qkv_call(_qkv_a_kernel, "norm_qkv_rope_a", i, x, mod, norm1_g[i],
                                    w_qv_t, w[:, D_MODEL:2 * D_MODEL], [], tables_a, tables_a_t)
            o = _attn_a_call(i, q_t, k, v_t,
                             (a_lam_q1[j], a_lam_k1[j], a_lam_q2[j], a_lam_k2[j]), a_subln_g[j])
            w_o = a_w_o[j]
        else:
            kv_w = B_N_KV * B_HEAD_DIM
            w = b_w_qkv[j].astype(BF16)
            w_qv_t = jnp.concatenate([w[:, :D_MODEL], w[:, D_MODEL + kv_w:]], axis=1).T
            q_gain = jnp.broadcast_to(b_qnorm_g[j][:, None], (B_HEAD_DIM, ROW_TILE))
            q_t, k, v_t = _qkv_call(_qkv_b_kernel, "norm_qkv_rope_b", i, x, mod, norm1_g[i],
                                    w_qv_t, w[:, D_MODEL:D_MODEL + kv_w],
                                    [q_gain, b_knorm_g[j].reshape(1, LANES)],
                                    tables_b, tables_b_t)
            o = _attn_b_call(q_t, k, v_t)
            w_o = b_w_o[j]
        x = _post_call(i, i == DEPTH - 1, x, o, mod, w_o.astype(BF16), norm2_g[i],
                       f_w_in[i].astype(BF16), f_w_out[i].astype(BF16), final_g)
    return x
```

```python
import functools
import math

import jax
import jax.numpy as jnp
from jax.experimental import pallas as pl
from jax.experimental.pallas import tpu as pltpu

D_MODEL = 1024
DEPTH = 2
GRID_W = 64
EPS = 1e-6
LOG2E = math.log2(math.e)

A_HEAD_DIM = 64
A_N_HEADS = D_MODEL // (2 * A_HEAD_DIM)
A_ROT = A_HEAD_DIM // 4
ROPE_THETA_1D = 500000.0

B_HEAD_DIM = 128
B_N_HEADS = D_MODEL // B_HEAD_DIM
B_N_KV = max(1, B_N_HEADS // 4)
B_GROUP = B_N_HEADS // B_N_KV
ROPE_THETA_AXIAL = 10000.0

D_FF = -(-8 * D_MODEL // (3 * 256)) * 256

LANES = 128
SUBLANES = 8
BF16_SUBLANES = 16
VMEM_LIMIT_BYTES = 56 * 1024 * 1024

ROW_TILE = 512
PROJ_WIDTH = 512
Q_LANES = 1024
Q_TILE_A = Q_LANES // 2
Q_TILE_B = Q_LANES // B_GROUP
KV_CHUNK = 256
FF_CHUNK = 256
V_ROWS = LANES + BF16_SUBLANES
PIPELINE_LAG = 2

BF16 = jnp.bfloat16
F32 = jnp.float32


def _lambda_init(layer_idx):
    return 0.8 - 0.6 * math.exp(-0.3 * layer_idx)


def _rms_scale(x):
    return jax.lax.rsqrt(jnp.mean(x * x, axis=-1, keepdims=True) + EPS)


def _mod_kernel(c_ref, w_ref, b_ref, o_ref):
    c = c_ref[...]
    cond = c * jax.nn.sigmoid(c)
    y = jnp.dot(cond.astype(BF16), w_ref[0].astype(BF16), preferred_element_type=F32)
    o_ref[0] = y + b_ref[0]


def _mod_call(c_pad, ada_w, ada_b):
    rows = c_pad.shape[0]
    n_chunks = 6
    return pl.pallas_call(
        _mod_kernel,
        grid=(DEPTH, n_chunks),
        in_specs=[
            pl.BlockSpec((rows, D_MODEL), lambda i, j: (0, 0)),
            pl.BlockSpec((1, D_MODEL, D_MODEL), lambda i, j: (i, 0, j)),
            pl.BlockSpec((1, 1, D_MODEL), lambda i, j: (i, 0, j)),
        ],
        out_specs=pl.BlockSpec((1, rows, D_MODEL), lambda i, j: (i, 0, j)),
        out_shape=jax.ShapeDtypeStruct((DEPTH, rows, 6 * D_MODEL), F32),
        compiler_params=pltpu.CompilerParams(
            dimension_semantics=("arbitrary", "arbitrary"),
            vmem_limit_bytes=VMEM_LIMIT_BYTES),
        name="adaln_mod",
    )(c_pad, ada_w, ada_b.reshape(DEPTH, 1, 6 * D_MODEL))


def _rope(x, cos_t, sin_fwd, sin_bwd, shift):
    fwd = pltpu.roll(x, LANES - shift, axis=1)
    bwd = pltpu.roll(x, shift, axis=1)
    return x * cos_t + fwd * sin_fwd + bwd * sin_bwd


def _modulated_norm(x, g, scale, shift):
    y = x * _rms_scale(x)
    return (y * g) * (1.0 + scale) + shift


def _transposed_projection(w_t_ref, h, block):
    n_rows = w_t_ref.shape[0]
    h_t = h.T
    for r in range(0, n_rows, block):
        yield r, jnp.dot(w_t_ref[r:min(r + block, n_rows), :], h_t, preferred_element_type=F32)


def _qkv_a_kernel(x_ref, g_ref, shift_ref, scale_ref, wqv_t_ref, wk_ref, cos_ref, sf_ref, sb_ref,
                  cos_t_ref, sin_t_ref, qt_ref, k_ref, vt_ref):
    h = _modulated_norm(x_ref[0], g_ref[...], scale_ref[0], shift_ref[0]).astype(BF16)

    half = A_ROT // 2
    cos_t, sf, sb = cos_ref[...], sf_ref[...], sb_ref[...]
    for c in range(0, D_MODEL, PROJ_WIDTH):
        y = jnp.dot(h, wk_ref[:, c:c + PROJ_WIDTH], preferred_element_type=F32)
        for u in range(0, PROJ_WIDTH, LANES):
            k = _rope(y[:, u:u + LANES], cos_t, sf, sb, half)
            k_ref[0, :, c + u:c + u + LANES] = k.astype(BF16)

    cos_rows, sin_rows = cos_t_ref[...], sin_t_ref[...]
    for r, y in _transposed_projection(wqv_t_ref, h, PROJ_WIDTH):
        if r < D_MODEL:
            pieces = []
            for base in range(0, PROJ_WIDTH, A_HEAD_DIM):
                lo, hi = y[base:base + half], y[base + half:base + A_ROT]
                pieces += [lo * cos_rows - hi * sin_rows, hi * cos_rows + lo * sin_rows,
                           y[base + A_ROT:base + A_HEAD_DIM]]
            q_t = jnp.concatenate(pieces, axis=0) * (A_HEAD_DIM ** -0.5 * LOG2E)
            qt_ref[0, r:r + PROJ_WIDTH, :] = q_t.astype(BF16)
        else:
            vt_ref[0, r - D_MODEL:r - D_MODEL + PROJ_WIDTH, :] = y.astype(BF16)


def _qkv_b_kernel(x_ref, g_ref, shift_ref, scale_ref, wqv_t_ref, wk_ref, qg_ref, kg_ref,
                  cos_ref, sf_ref, sb_ref, cos_t_ref, sin_t_ref, qt_ref, k_ref, vt_ref):
    h = _modulated_norm(x_ref[0], g_ref[...], scale_ref[0], shift_ref[0]).astype(BF16)

    quarter = B_HEAD_DIM // 4
    cos_t, sf, sb = cos_ref[...], sf_ref[...], sb_ref[...]
    y = jnp.dot(h, wk_ref[...], preferred_element_type=F32)
    for u in range(0, wk_ref.shape[1], LANES):
        k = y[:, u:u + LANES]
        k = (k * _rms_scale(k)) * kg_ref[...]
        k = _rope(k, cos_t, sf, sb, quarter)
        k_ref[0, :, u:u + LANES] = k.astype(BF16)

    cos_rows, sin_rows, q_gain = cos_t_ref[...], sin_t_ref[...], qg_ref[...]
    for r, y in _transposed_projection(wqv_t_ref, h, PROJ_WIDTH):
        if r < D_MODEL:
            for base in range(0, PROJ_WIDTH, B_HEAD_DIM):
                q = y[base:base + B_HEAD_DIM]
                inv = jax.lax.rsqrt(jnp.mean(q * q, axis=0, keepdims=True) + EPS)
                q = (q * inv) * q_gain
                partner = jnp.concatenate(
                    [q[quarter:2 * quarter], q[:quarter], q[3 * quarter:], q[2 * quarter:3 * quarter]],
                    axis=0)
                q = (q * cos_rows + partner * sin_rows) * (B_HEAD_DIM ** -0.5 * LOG2E)
                qt_ref[0, r + base:r + base + B_HEAD_DIM, :] = q.astype(BF16)
        else:
            vt_ref[0, r - D_MODEL:r - D_MODEL + y.shape[0], :] = y.astype(BF16)


def _mod_spec(layer, chunk):
    return pl.BlockSpec((1, 1, D_MODEL), lambda b, i: (layer * SUBLANES + b, 0, chunk))


def _const_spec(shape):
    return pl.BlockSpec(shape, lambda *_: (0,) * len(shape), pipeline_mode=pl.Buffered(1))


def _qkv_call(kernel_fn, name, layer, x, mod, norm_g, w_qv_t, w_k, extra, tables, tables_t):
    batch, seq, _ = x.shape
    n_q = D_MODEL
    n_v = w_qv_t.shape[0] - n_q
    n_k = w_k.shape[1]
    table_spec = pl.BlockSpec((ROW_TILE, LANES), lambda b, i: (i, 0))
    table_t_spec = pl.BlockSpec((tables_t[0].shape[0], ROW_TILE), lambda b, i: (0, i))
    in_specs = [
        pl.BlockSpec((1, ROW_TILE, D_MODEL), lambda b, i: (b, i, 0)),
        _const_spec((1, D_MODEL)),
        _mod_spec(layer, 0),
        _mod_spec(layer, 1),
        _const_spec(w_qv_t.shape),
        _const_spec(w_k.shape),
    ] + [_const_spec(e.shape) for e in extra] + [table_spec] * 3 + [table_t_spec] * 2
    return pl.pallas_call(
        kernel_fn,
        grid=(batch, seq // ROW_TILE),
        in_specs=in_specs,
        out_specs=[
            pl.BlockSpec((1, n_q, ROW_TILE), lambda b, i: (b, 0, i)),
            pl.BlockSpec((1, ROW_TILE, n_k), lambda b, i: (b, i, 0)),
            pl.BlockSpec((1, n_v, ROW_TILE), lambda b, i: (b, 0, i)),
        ],
        out_shape=[
            jax.ShapeDtypeStruct((batch, n_q, seq), BF16),
            jax.ShapeDtypeStruct((batch, seq, n_k), BF16),
            jax.ShapeDtypeStruct((batch, n_v, seq), BF16),
        ],
        compiler_params=pltpu.CompilerParams(
            dimension_semantics=("arbitrary", "arbitrary"),
            vmem_limit_bytes=VMEM_LIMIT_BYTES),
        name=name,
    )(x, norm_g.reshape(1, D_MODEL), mod, mod, w_qv_t, w_k, *extra, *tables, *tables_t)


def _scores_and_weighted_sum(q_t, k_ref, vt_ref, write, read, seq):
    s_w_ref, m_w_ref = write
    s_r_ref, m_r_ref = read
    n = q_t.shape[1]
    m_prev = m_r_ref[0:1, :]
    col_max = None
    acc = None
    for j in range(seq // KV_CHUNK):
        rows = slice(j * KV_CHUNK, (j + 1) * KV_CHUNK)
        s = jnp.dot(k_ref[0, rows, :], q_t, preferred_element_type=F32)
        s_w_ref[rows, :] = s
        part = jnp.max(s.reshape(KV_CHUNK // SUBLANES, SUBLANES, n), axis=0)
        col_max = part if col_max is None else jnp.maximum(col_max, part)

        p = jnp.exp2(s_r_ref[rows, :] - m_prev).astype(BF16)
        pv = jnp.dot(vt_ref[:, rows], p, preferred_element_type=F32)
        acc = pv if acc is None else acc + pv
    m = jnp.max(col_max, axis=0, keepdims=True)
    m_w_ref[...] = jnp.broadcast_to(m, (SUBLANES, n))
    return acc


def _pipelined_step(tiles_per_head, v_ref, vt_ref, buffers, step_fn):
    t = pl.program_id(0)
    s1_ref, m1_ref, acc1_ref = buffers[1]

    @pl.when(t == 0)
    def _():
        s1_ref[...] = jnp.zeros(s1_ref.shape, F32)
        m1_ref[...] = jnp.zeros(m1_ref.shape, F32)
        acc1_ref[...] = jnp.ones(acc1_ref.shape, F32)

    @pl.when(jnp.maximum(t - 1, 0) % tiles_per_head == 0)
    def _():
        vt_ref[:LANES, :] = v_ref[0]
        vt_ref[LANES:, :] = jnp.ones((V_ROWS - LANES, v_ref.shape[2]), BF16)

    @pl.when(t % 2 == 0)
    def _():
        step_fn(buffers[0], buffers[1])

    @pl.when(t % 2 == 1)
    def _():
        step_fn(buffers[1], buffers[0])


def _normalised(acc, cols):
    return acc[:LANES, cols] / acc[LANES:LANES + 1, cols]


def _attn_a_kernel(lam_init, tiles_per_head, q_ref, k_ref, v_ref, lq1_ref, lk1_ref, lq2_ref,
                   lk2_ref, g_ref, o_ref, vt_ref, *scratch):
    seq = k_ref.shape[1]
    tq = q_ref.shape[2]

    def step(write, read):
        acc_done = read[2][...]
        lam = (jnp.exp(jnp.sum(lq1_ref[...] * lk1_ref[...], axis=1, keepdims=True))
               - jnp.exp(jnp.sum(lq2_ref[...] * lk2_ref[...], axis=1, keepdims=True))
               + lam_init)
        o_t = (_normalised(acc_done, slice(0, tq))
               - lam * _normalised(acc_done, slice(tq, 2 * tq)))
        o = o_t.T
        o = (o * _rms_scale(o)) * g_ref[...]
        o_ref[0] = (o * (1.0 - lam_init)).astype(BF16)

        q = q_ref[0]
        row = jax.lax.broadcasted_iota(jnp.int32, q.shape, 0)
        zero = jnp.zeros_like(q)
        q_t = jnp.concatenate([jnp.where(row < A_HEAD_DIM, q, zero),
                               jnp.where(row >= A_HEAD_DIM, q, zero)], axis=1)
        write[2][...] = _scores_and_weighted_sum(q_t, k_ref, vt_ref, write[:2], read[:2], seq)

    _pipelined_step(tiles_per_head, v_ref, vt_ref, (scratch[:3], scratch[3:]), step)


def _attn_b_kernel(tiles_per_head, q_ref, k_ref, v_ref, o_ref, vt_ref, *scratch):
    seq = k_ref.shape[1]
    tq = q_ref.shape[2]

    def step(write, read):
        acc_done = read[2][...]
        for g in range(B_GROUP):
            o_t = _normalised(acc_done, slice(g * tq, (g + 1) * tq))
            o_ref[0, :, g * LANES:(g + 1) * LANES] = o_t.T.astype(BF16)

        q = q_ref[0]
        q_t = jnp.concatenate([q[g * LANES:(g + 1) * LANES, :] for g in range(B_GROUP)], axis=1)
        write[2][...] = _scores_and_weighted_sum(q_t, k_ref, vt_ref, write[:2], read[:2], seq)

    _pipelined_step(tiles_per_head, v_ref, vt_ref, (scratch[:3], scratch[3:]), step)


def _attn_scratch(seq):
    buffer_set = [pltpu.VMEM((seq, Q_LANES), F32),
                  pltpu.VMEM((SUBLANES, Q_LANES), F32),
                  pltpu.VMEM((V_ROWS, Q_LANES), F32)]
    return [pltpu.VMEM((V_ROWS, seq), BF16)] + buffer_set + buffer_set


def _tile_maps(n_tiles, n_heads, tiles_per_head):
    def split(tile):
        return (tile // (n_heads * tiles_per_head), (tile // tiles_per_head) % n_heads,
                tile % tiles_per_head)

    def lagged(lag):
        return lambda t: split(jnp.clip(t - lag, 0, n_tiles - 1))

    return lagged


def _attn_specs(seq, q_block, o_block, lagged):
    current, previous, finished = lagged(0), lagged(1), lagged(PIPELINE_LAG)

    def q_map(t):
        b, h, i = current(t)
        return (b, h, i)

    def k_map(t):
        b, h, _ = current(t)
        return (b, 0, h)

    def v_map(t):
        b, h, _ = previous(t)
        return (b, h, 0)

    def o_map(t):
        b, h, i = finished(t)
        return (b, i, h)

    in_specs = [pl.BlockSpec(q_block, q_map),
                pl.BlockSpec((1, seq, LANES), k_map),
                pl.BlockSpec((1, LANES, seq), v_map)]
    return in_specs, pl.BlockSpec(o_block, o_map)


def _attn_a_call(layer, q_t, k, v_t, lam_params, subln_g):
    batch, seq, _ = k.shape
    tiles_per_head = seq // Q_TILE_A
    n_tiles = batch * A_N_HEADS * tiles_per_head
    in_specs, out_spec = _attn_specs(seq, (1, LANES, Q_TILE_A), (1, Q_TILE_A, LANES),
                                     _tile_maps(n_tiles, A_N_HEADS, tiles_per_head))
    kernel_fn = functools.partial(_attn_a_kernel, _lambda_init(layer), tiles_per_head)
    small = [p.reshape(1, A_HEAD_DIM) for p in lam_params]
    return pl.pallas_call(
        kernel_fn,
        grid=(n_tiles + PIPELINE_LAG,),
        in_specs=in_specs + [_const_spec((1, A_HEAD_DIM))] * 4 + [_const_spec((1, LANES))],
        out_specs=out_spec,
        out_shape=jax.ShapeDtypeStruct((batch, seq, D_MODEL), BF16),
        scratch_shapes=_attn_scratch(seq),
        compiler_params=pltpu.CompilerParams(
            dimension_semantics=("arbitrary",),
            vmem_limit_bytes=VMEM_LIMIT_BYTES),
        name="diff_attention",
    )(q_t, k, v_t, *small, subln_g.reshape(1, LANES))


def _attn_b_call(q_t, k, v_t):
    batch, seq, _ = k.shape
    group_w = B_GROUP * B_HEAD_DIM
    tiles_per_head = seq // Q_TILE_B
    n_tiles = batch * B_N_KV * tiles_per_head
    in_specs, out_spec = _attn_specs(seq, (1, group_w, Q_TILE_B), (1, Q_TILE_B, group_w),
                                     _tile_maps(n_tiles, B_N_KV, tiles_per_head))
    return pl.pallas_call(
        functools.partial(_attn_b_kernel, tiles_per_head),
        grid=(n_tiles + PIPELINE_LAG,),
        in_specs=in_specs,
        out_specs=out_spec,
        out_shape=jax.ShapeDtypeStruct((batch, seq, D_MODEL), BF16),
        scratch_shapes=_attn_scratch(seq),
        compiler_params=pltpu.CompilerParams(
            dimension_semantics=("arbitrary",),
            vmem_limit_bytes=VMEM_LIMIT_BYTES),
        name="axial_gqa_attention",
    )(q_t, k, v_t)


def _post_kernel(final_norm, x_ref, o_ref, wo_ref, gate1_ref, g2_ref, shift2_ref, scale2_ref,
                 gate2_ref, win_ref, wout_ref, fg_ref, out_ref, act_ref):
    y = jnp.dot(o_ref[0], wo_ref[...], preferred_element_type=F32)
    x1 = x_ref[0] + gate1_ref[0] * y
    out_ref[0] = x1
    h = _modulated_norm(x1, g2_ref[...], scale2_ref[0], shift2_ref[0]).astype(BF16)
    for c in range(D_FF // FF_CHUNK):
        cols = slice(c * FF_CHUNK, (c + 1) * FF_CHUNK)
        up_cols = slice(D_FF + c * FF_CHUNK, D_FF + (c + 1) * FF_CHUNK)
        gate = jnp.dot(h, win_ref[:, cols], preferred_element_type=F32)
        up = jnp.dot(h, win_ref[:, up_cols], preferred_element_type=F32)
        act_ref[:, cols] = ((gate * jax.nn.sigmoid(gate)) * up).astype(BF16)
    f = jnp.dot(act_ref[...], wout_ref[...], preferred_element_type=F32)
    x2 = out_ref[0] + gate2_ref[0] * f
    if final_norm:
        x2 = (x2 * _rms_scale(x2)) * fg_ref[...]
    out_ref[0] = x2


def _post_call(layer, final_norm, x, o, mod, w_o, norm2_g, w_in, w_out, final_g):
    batch, seq, _ = x.shape
    row_spec = pl.BlockSpec((1, ROW_TILE, D_MODEL), lambda b, i: (b, i, 0))
    return pl.pallas_call(
        functools.partial(_post_kernel, final_norm),
        grid=(batch, seq // ROW_TILE),
        in_specs=[
            row_spec,
            row_spec,
            _const_spec((D_MODEL, D_MODEL)),
            _mod_spec(layer, 2),
            _const_spec((1, D_MODEL)),
            _mod_spec(layer, 3),
            _mod_spec(layer, 4),
            _mod_spec(layer, 5),
            _const_spec((D_MODEL, 2 * D_FF)),
            _const_spec((D_FF, D_MODEL)),
            _const_spec((1, D_MODEL)),
        ],
        out_specs=row_spec,
        out_shape=jax.ShapeDtypeStruct((batch, seq, D_MODEL), F32),
        scratch_shapes=[pltpu.VMEM((ROW_TILE, D_FF), BF16)],
        compiler_params=pltpu.CompilerParams(
            dimension_semantics=("arbitrary", "arbitrary"),
            vmem_limit_bytes=VMEM_LIMIT_BYTES),
        name="wo_ffn",
    )(x, o, w_o, mod, norm2_g.reshape(1, D_MODEL), mod, mod, mod, w_in, w_out,
      final_g.reshape(1, D_MODEL))


def _rope_angles(pos, half, theta):
    freqs = theta ** (-jnp.arange(half, dtype=F32) / half)
    ang = pos.astype(F32)[:, None] * freqs[None, :]
    return jnp.cos(ang), jnp.sin(ang)


def _rope_tables(pos_per_lane_group, half, theta, period, seq):
    cos_cols, sf_cols, sb_cols = [], [], []
    for pos in pos_per_lane_group:
        c, s = _rope_angles(pos, half, theta)
        z = jnp.zeros_like(s)
        cos_cols += [c, c]
        sf_cols += [-s, z]
        sb_cols += [z, s]
    rest = period - 2 * half * len(pos_per_lane_group)
    if rest:
        cos_cols.append(jnp.ones((seq, rest), F32))
        sf_cols.append(jnp.zeros((seq, rest), F32))
        sb_cols.append(jnp.zeros((seq, rest), F32))
    reps = LANES // period
    return tuple(jnp.tile(jnp.concatenate(cols, axis=1), (1, reps))
                 for cols in (cos_cols, sf_cols, sb_cols))


def kernel(x, c, ada_w, ada_b, norm1_g, norm2_g, a_w_qkv, a_w_o, a_lam_q1, a_lam_k1, a_lam_q2,
           a_lam_k2, a_subln_g, b_w_qkv, b_w_o, b_qnorm_g, b_knorm_g, f_w_in, f_w_out, final_g):
    batch, seq, _ = x.shape
    assert batch <= SUBLANES and seq % ROW_TILE == 0 and seq % KV_CHUNK == 0

    t = jnp.arange(seq, dtype=jnp.int32)
    row_pos, col_pos = t // GRID_W, t % GRID_W
    tables_a = _rope_tables([t], A_ROT // 2, ROPE_THETA_1D, A_HEAD_DIM, seq)
    tables_b = _rope_tables([row_pos, col_pos], B_HEAD_DIM // 4, ROPE_THETA_AXIAL,
                            B_HEAD_DIM, seq)
    cos_a, sin_a = _rope_angles(t, A_ROT // 2, ROPE_THETA_1D)
    tables_a_t = (cos_a.T, sin_a.T)
    cos_r, sin_r = _rope_angles(row_pos, B_HEAD_DIM // 4, ROPE_THETA_AXIAL)
    cos_c, sin_c = _rope_angles(col_pos, B_HEAD_DIM // 4, ROPE_THETA_AXIAL)
    tables_b_t = (jnp.concatenate([cos_r, cos_r, cos_c, cos_c], axis=1).T,
                  jnp.concatenate([-sin_r, sin_r, -sin_c, sin_c], axis=1).T)

    c_pad = jnp.zeros((SUBLANES, D_MODEL), F32).at[:batch].set(c)
    mod = _mod_call(c_pad, ada_w, ada_b).reshape(DEPTH * SUBLANES, 1, 6 * D_MODEL)

    for i in range(DEPTH):
        j = i // 2
        if i % 2 == 0:
            w = a_w_qkv[j].astype(BF16)
            w_qv_t = jnp.concatenate([w[:, :D_MODEL], w[:, 2 * D_MODEL:]], axis=1).T
            q_t, k, v_t = _qkv_call(_qkv_a_kernel, "norm_qkv_rope_a", i, x, mod, norm1_g[i],
                                    w_qv_t, w[:, D_MODEL:2 * D_MODEL], [], tables_a, tables_a_t)
            o = _attn_a_call(i, q_t, k, v_t,
                             (a_lam_q1[j], a_lam_k1[j], a_lam_q2[j], a_lam_k2[j]), a_subln_g[j])
            w_o = a_w_o[j]
        else:
            kv_w = B_N_KV * B_HEAD_DIM
            w = b_w_qkv[j].astype(BF16)
            w_qv_t = jnp.concatenate([w[:, :D_MODEL], w[:, D_MODEL + kv_w:]], axis=1).T
            q_gain = jnp.broadcast_to(b_qnorm_g[j][:, None], (B_HEAD_DIM, ROW_TILE))
            q_t, k, v_t = _qkv_call(_qkv_b_kernel, "norm_qkv_rope_b", i, x, mod, norm1_g[i],
                                    w_qv_t, w[:, D_MODEL:D_MODEL + kv_w],
                                    [q_gain, b_knorm_g[j].reshape(1, LANES)],
                                    tables_b, tables_b_t)
            o = _attn_b_call(q_t, k, v_t)
            w_o = b_w_o[j]
        x = _post_call(i, i == DEPTH - 1, x, o, mod, w_o.astype(BF16), norm2_g[i],
                       f_w_in[i].astype(BF16), f_w_out[i].astype(BF16), final_g)
    return x
```

```python
import functools
import math

import jax
import jax.numpy as jnp
from jax.experimental import pallas as pl
from jax.experimental.pallas import tpu as pltpu

D_MODEL = 1024
DEPTH = 2
GRID_W = 64
EPS = 1e-6
LOG2E = math.log2(math.e)

A_HEAD_DIM = 64
A_N_HEADS = D_MODEL // (2 * A_HEAD_DIM)
A_ROT = A_HEAD_DIM // 4
ROPE_THETA_1D = 500000.0

B_HEAD_DIM = 128
B_N_HEADS = D_MODEL // B_HEAD_DIM
B_N_KV = max(1, B_N_HEADS // 4)
B_GROUP = B_N_HEADS // B_N_KV
ROPE_THETA_AXIAL = 10000.0

D_FF = -(-8 * D_MODEL // (3 * 256)) * 256

LANES = 128
SUBLANES = 8
VMEM_LIMIT_BYTES = 56 * 1024 * 1024

ROW_TILE = 512
PROJ_WIDTH = 512
Q_LANES = 1024
Q_TILE_A = Q_LANES // 2
Q_TILE_B = Q_LANES // B_GROUP
KV_CHUNK = 256
FF_CHUNK = 256
ACC_ROWS = LANES + SUBLANES
PIPELINE_LAG = 2

BF16 = jnp.bfloat16
F32 = jnp.float32


def _lambda_init(layer_idx):
    return 0.8 - 0.6 * math.exp(-0.3 * layer_idx)


def _rms_scale(x):
    return jax.lax.rsqrt(jnp.mean(x * x, axis=-1, keepdims=True) + EPS)


def _mod_kernel(c_ref, w_ref, b_ref, o_ref):
    c = c_ref[...]
    cond = c * jax.nn.sigmoid(c)
    y = jnp.dot(cond.astype(BF16), w_ref[0].astype(BF16), preferred_element_type=F32)
    o_ref[0] = y + b_ref[0]


def _mod_call(c, ada_w, ada_b):
    rows = c.shape[0]
    n_chunks = 6
    return pl.pallas_call(
        _mod_kernel,
        grid=(DEPTH, n_chunks),
        in_specs=[
            pl.BlockSpec((rows, D_MODEL), lambda i, j: (0, 0)),
            pl.BlockSpec((1, D_MODEL, D_MODEL), lambda i, j: (i, 0, j)),
            pl.BlockSpec((1, 1, D_MODEL), lambda i, j: (i, 0, j)),
        ],
        out_specs=pl.BlockSpec((1, rows, D_MODEL), lambda i, j: (i, 0, j)),
        out_shape=jax.ShapeDtypeStruct((DEPTH, rows, 6 * D_MODEL), F32),
        compiler_params=pltpu.CompilerParams(
            dimension_semantics=("arbitrary", "arbitrary"),
            vmem_limit_bytes=VMEM_LIMIT_BYTES),
        name="adaln_mod",
    )(c, ada_w, ada_b.reshape(DEPTH, 1, 6 * D_MODEL))


def _rope(x, cos_t, sin_fwd, sin_bwd, shift):
    fwd = pltpu.roll(x, LANES - shift, axis=1)
    bwd = pltpu.roll(x, shift, axis=1)
    return x * cos_t + fwd * sin_fwd + bwd * sin_bwd


def _modulated_norm(x, g, scale, shift):
    y = x * _rms_scale(x)
    return (y * g) * (1.0 + scale) + shift


def _transposed_projection(w_t_ref, h, block):
    n_rows = w_t_ref.shape[0]
    h_t = h.T
    for r in range(0, n_rows, block):
        yield r, jnp.dot(w_t_ref[r:min(r + block, n_rows), :], h_t, preferred_element_type=F32)


def _qkv_a_kernel(x_ref, g_ref, shift_ref, scale_ref, wqv_t_ref, wk_ref, cos_ref, sf_ref, sb_ref,
                  cos_t_ref, sin_t_ref, qt_ref, k_ref, vt_ref):
    h = _modulated_norm(x_ref[0], g_ref[...], scale_ref[0], shift_ref[0]).astype(BF16)

    half = A_ROT // 2
    cos_t, sf, sb = cos_ref[...], sf_ref[...], sb_ref[...]
    for c in range(0, D_MODEL, PROJ_WIDTH):
        y = jnp.dot(h, wk_ref[:, c:c + PROJ_WIDTH], preferred_element_type=F32)
        for u in range(0, PROJ_WIDTH, LANES):
            k = _rope(y[:, u:u + LANES], cos_t, sf, sb, half)
            k_ref[0, :, c + u:c + u + LANES] = k.astype(BF16)

    cos_rows, sin_rows = cos_t_ref[...], sin_t_ref[...]
    for r, y in _transposed_projection(wqv_t_ref, h, PROJ_WIDTH):
        if r < D_MODEL:
            pieces = []
            for base in range(0, PROJ_WIDTH, A_HEAD_DIM):
                lo, hi = y[base:base + half], y[base + half:base + A_ROT]
                pieces += [lo * cos_rows - hi * sin_rows, hi * cos_rows + lo * sin_rows,
                           y[base + A_ROT:base + A_HEAD_DIM]]
            q_t = jnp.concatenate(pieces, axis=0) * (A_HEAD_DIM ** -0.5 * LOG2E)
            qt_ref[0, r:r + PROJ_WIDTH, :] = q_t.astype(BF16)
        else:
            vt_ref[0, r - D_MODEL:r - D_MODEL + PROJ_WIDTH, :] = y.astype(BF16)


def _qkv_b_kernel(x_ref, g_ref, shift_ref, scale_ref, wqv_t_ref, wk_ref, qg_ref, kg_ref,
                  cos_ref, sf_ref, sb_ref, cos_t_ref, sin_t_ref, qt_ref, k_ref, vt_ref):
    h = _modulated_norm(x_ref[0], g_ref[...], scale_ref[0], shift_ref[0]).astype(BF16)

    quarter = B_HEAD_DIM // 4
    cos_t, sf, sb = cos_ref[...], sf_ref[...], sb_ref[...]
    y = jnp.dot(h, wk_ref[...], preferred_element_type=F32)
    for u in range(0, wk_ref.shape[1], LANES):
        k = y[:, u:u + LANES]
        k = (k * _rms_scale(k)) * kg_ref[...]
        k = _rope(k, cos_t, sf, sb, quarter)
        k_ref[0, :, u:u + LANES] = k.astype(BF16)

    cos_rows, sin_rows, q_gain = cos_t_ref[...], sin_t_ref[...], qg_ref[...]
    for r, y in _transposed_projection(wqv_t_ref, h, PROJ_WIDTH):
        if r < D_MODEL:
            for base in range(0, PROJ_WIDTH, B_HEAD_DIM):
                q = y[base:base + B_HEAD_DIM]
                inv = jax.lax.rsqrt(jnp.mean(q * q, axis=0, keepdims=True) + EPS)
                q = (q * inv) * q_gain
                partner = jnp.concatenate(
                    [q[quarter:2 * quarter], q[:quarter], q[3 * quarter:], q[2 * quarter:3 * quarter]],
                    axis=0)
                q = (q * cos_rows + partner * sin_rows) * (B_HEAD_DIM ** -0.5 * LOG2E)
                qt_ref[0, r + base:r + base + B_HEAD_DIM, :] = q.astype(BF16)
        else:
            vt_ref[0, r - D_MODEL:r - D_MODEL + y.shape[0], :] = y.astype(BF16)


def _mod_spec(mod, layer, chunk):
    batch = mod.shape[0] // DEPTH
    return pl.BlockSpec((1, 1, D_MODEL), lambda b, i: (layer * batch + b, 0, chunk))


def _const_spec(shape):
    return pl.BlockSpec(shape, lambda *_: (0,) * len(shape), pipeline_mode=pl.Buffered(1))


def _layer_spec(stacked, layer):
    return pl.BlockSpec((1,) + stacked.shape[1:], lambda *_: (layer, 0, 0),
                        pipeline_mode=pl.Buffered(1))


def _qkv_call(kernel_fn, name, layer, x, mod, norm_g, w_qv_t, w_k, extra, tables, tables_t):
    batch, seq, _ = x.shape
    n_q = D_MODEL
    n_v = w_qv_t.shape[0] - n_q
    n_k = w_k.shape[1]
    table_spec = pl.BlockSpec((ROW_TILE, LANES), lambda b, i: (i, 0))
    table_t_spec = pl.BlockSpec((tables_t[0].shape[0], ROW_TILE), lambda b, i: (0, i))
    in_specs = [
        pl.BlockSpec((1, ROW_TILE, D_MODEL), lambda b, i: (b, i, 0)),
        _const_spec((1, D_MODEL)),
        _mod_spec(mod, layer, 0),
        _mod_spec(mod, layer, 1),
        _const_spec(w_qv_t.shape),
        _const_spec(w_k.shape),
    ] + [_const_spec(e.shape) for e in extra] + [table_spec] * 3 + [table_t_spec] * 2
    return pl.pallas_call(
        kernel_fn,
        grid=(batch, seq // ROW_TILE),
        in_specs=in_specs,
        out_specs=[
            pl.BlockSpec((1, n_q, ROW_TILE), lambda b, i: (b, 0, i)),
            pl.BlockSpec((1, ROW_TILE, n_k), lambda b, i: (b, i, 0)),
            pl.BlockSpec((1, n_v, ROW_TILE), lambda b, i: (b, 0, i)),
        ],
        out_shape=[
            jax.ShapeDtypeStruct((batch, n_q, seq), BF16),
            jax.ShapeDtypeStruct((batch, seq, n_k), BF16),
            jax.ShapeDtypeStruct((batch, n_v, seq), BF16),
        ],
        compiler_params=pltpu.CompilerParams(
            dimension_semantics=("arbitrary", "arbitrary"),
            vmem_limit_bytes=VMEM_LIMIT_BYTES),
        name=name,
    )(x, norm_g.reshape(1, D_MODEL), mod, mod, w_qv_t, w_k, *extra, *tables, *tables_t)


def _scores_and_weighted_sum(q_t, k_ref, vt_ref, write, read, seq):
    s_w_ref, m_w_ref = write
    s_r_ref, m_r_ref = read
    n = q_t.shape[1]
    groups = KV_CHUNK // SUBLANES
    m_prev = m_r_ref[0:1, :]
    col_max = None
    col_sum = None
    acc = None
    for j in range(seq // KV_CHUNK):
        rows = slice(j * KV_CHUNK, (j + 1) * KV_CHUNK)
        s = jnp.dot(k_ref[0, rows, :], q_t, preferred_element_type=F32)
        s_w_ref[rows, :] = s
        part = jnp.max(s.reshape(groups, SUBLANES, n), axis=0)
        col_max = part if col_max is None else jnp.maximum(col_max, part)

        p = jnp.exp2(s_r_ref[rows, :] - m_prev)
        part = jnp.sum(p.reshape(groups, SUBLANES, n), axis=0)
        col_sum = part if col_sum is None else col_sum + part
        pv = jnp.dot(vt_ref[0, :, rows], p.astype(BF16), preferred_element_type=F32)
        acc = pv if acc is None else acc + pv
    m = jnp.max(col_max, axis=0, keepdims=True)
    m_w_ref[...] = jnp.broadcast_to(m, (SUBLANES, n))
    return jnp.concatenate([acc, col_sum], axis=0)


def _pipelined_step(buffers, step_fn):
    t = pl.program_id(0)
    s1_ref, m1_ref, acc1_ref = buffers[1]

    @pl.when(t == 0)
    def _():
        s1_ref[...] = jnp.zeros(s1_ref.shape, F32)
        m1_ref[...] = jnp.zeros(m1_ref.shape, F32)
        acc1_ref[...] = jnp.ones(acc1_ref.shape, F32)

    @pl.when(t % 2 == 0)
    def _():
        step_fn(buffers[0], buffers[1])

    @pl.when(t % 2 == 1)
    def _():
        step_fn(buffers[1], buffers[0])


def _normalised(acc, cols):
    return acc[:LANES, cols] / jnp.sum(acc[LANES:, cols], axis=0, keepdims=True)


def _attn_a_kernel(lam_init, q_ref, k_ref, vt_ref, lq1_ref, lk1_ref, lq2_ref, lk2_ref, g_ref,
                   o_ref, *scratch):
    seq = k_ref.shape[1]
    tq = q_ref.shape[2]

    def step(write, read):
        acc_done = read[2][...]
        lam = (jnp.exp(jnp.sum(lq1_ref[...] * lk1_ref[...], axis=1, keepdims=True))
               - jnp.exp(jnp.sum(lq2_ref[...] * lk2_ref[...], axis=1, keepdims=True))
               + lam_init)
        o_t = (_normalised(acc_done, slice(0, tq))
               - lam * _normalised(acc_done, slice(tq, 2 * tq)))
        o = o_t.T
        o = (o * _rms_scale(o)) * g_ref[...]
        o_ref[0] = (o * (1.0 - lam_init)).astype(BF16)

        q = q_ref[0]
        row = jax.lax.broadcasted_iota(jnp.int32, q.shape, 0)
        zero = jnp.zeros_like(q)
        q_t = jnp.concatenate([jnp.where(row < A_HEAD_DIM, q, zero),
                               jnp.where(row >= A_HEAD_DIM, q, zero)], axis=1)
        write[2][...] = _scores_and_weighted_sum(q_t, k_ref, vt_ref, write[:2], read[:2], seq)

    _pipelined_step((scratch[:3], scratch[3:]), step)


def _attn_b_kernel(q_ref, k_ref, vt_ref, o_ref, *scratch):
    seq = k_ref.shape[1]
    tq = q_ref.shape[2]

    def step(write, read):
        acc_done = read[2][...]
        for g in range(B_GROUP):
            o_t = _normalised(acc_done, slice(g * tq, (g + 1) * tq))
            o_ref[0, :, g * LANES:(g + 1) * LANES] = o_t.T.astype(BF16)

        q = q_ref[0]
        q_t = jnp.concatenate([q[g * LANES:(g + 1) * LANES, :] for g in range(B_GROUP)], axis=1)
        write[2][...] = _scores_and_weighted_sum(q_t, k_ref, vt_ref, write[:2], read[:2], seq)

    _pipelined_step((scratch[:3], scratch[3:]), step)


def _attn_scratch(seq):
    buffer_set = [pltpu.VMEM((seq, Q_LANES), F32),
                  pltpu.VMEM((SUBLANES, Q_LANES), F32),
                  pltpu.VMEM((ACC_ROWS, Q_LANES), F32)]
    return buffer_set + buffer_set


def _attn_call(kernel_fn, name, q_t, k, v_t, n_heads, q_rows, tile, extra):
    batch, seq, _ = k.shape
    tiles_per_head = seq // tile
    n_tiles = batch * n_heads * tiles_per_head

    def split(t):
        t = jnp.clip(t, 0, n_tiles - 1)
        return (t // (n_heads * tiles_per_head), (t // tiles_per_head) % n_heads,
                t % tiles_per_head)

    def q_map(t):
        b, h, i = split(t)
        return (b, h, i)

    def k_map(t):
        b, h, _ = split(t)
        return (b, 0, h)

    def v_map(t):
        b, h, _ = split(t - 1)
        return (b, h, 0)

    def o_map(t):
        b, h, i = split(t - PIPELINE_LAG)
        return (b, i, h)

    return pl.pallas_call(
        kernel_fn,
        grid=(n_tiles + PIPELINE_LAG,),
        in_specs=[pl.BlockSpec((1, q_rows, tile), q_map),
                  pl.BlockSpec((1, seq, LANES), k_map),
                  pl.BlockSpec((1, LANES, seq), v_map)] + [_const_spec(e.shape) for e in extra],
        out_specs=pl.BlockSpec((1, tile, D_MODEL // n_heads), o_map),
        out_shape=jax.ShapeDtypeStruct((batch, seq, D_MODEL), BF16),
        scratch_shapes=_attn_scratch(seq),
        compiler_params=pltpu.CompilerParams(
            dimension_semantics=("arbitrary",),
            vmem_limit_bytes=VMEM_LIMIT_BYTES),
        name=name,
    )(q_t, k, v_t, *extra)


def _attn_a_call(layer, q_t, k, v_t, lam_params, subln_g):
    extra = [p.reshape(1, A_HEAD_DIM) for p in lam_params] + [subln_g.reshape(1, LANES)]
    return _attn_call(functools.partial(_attn_a_kernel, _lambda_init(layer)), "diff_attention",
                      q_t, k, v_t, A_N_HEADS, LANES, Q_TILE_A, extra)


def _attn_b_call(q_t, k, v_t):
    return _attn_call(_attn_b_kernel, "axial_gqa_attention", q_t, k, v_t, B_N_KV,
                      B_GROUP * B_HEAD_DIM, Q_TILE_B, [])


def _post_kernel(final_norm, x_ref, o_ref, wo_ref, gate1_ref, g2_ref, shift2_ref, scale2_ref,
                 gate2_ref, win_ref, wout_ref, fg_ref, out_ref, act_ref):
    y = jnp.dot(o_ref[0], wo_ref[0].astype(BF16), preferred_element_type=F32)
    x1 = x_ref[0] + gate1_ref[0] * y
    out_ref[0] = x1
    h = _modulated_norm(x1, g2_ref[...], scale2_ref[0], shift2_ref[0]).astype(BF16)
    for c in range(D_FF // FF_CHUNK):
        cols = slice(c * FF_CHUNK, (c + 1) * FF_CHUNK)
        up_cols = slice(D_FF + c * FF_CHUNK, D_FF + (c + 1) * FF_CHUNK)
        gate = jnp.dot(h, win_ref[0, :, cols].astype(BF16), preferred_element_type=F32)
        up = jnp.dot(h, win_ref[0, :, up_cols].astype(BF16), preferred_element_type=F32)
        act_ref[:, cols] = ((gate * jax.nn.sigmoid(gate)) * up).astype(BF16)
    f = jnp.dot(act_ref[...], wout_ref[0].astype(BF16), preferred_element_type=F32)
    x2 = out_ref[0] + gate2_ref[0] * f
    if final_norm:
        x2 = (x2 * _rms_scale(x2)) * fg_ref[...]
    out_ref[0] = x2


def _post_call(layer, final_norm, x, o, mod, w_o, w_o_layer, norm2_g, w_in, w_out, final_g):
    batch, seq, _ = x.shape
    row_spec = pl.BlockSpec((1, ROW_TILE, D_MODEL), lambda b, i: (b, i, 0))
    return pl.pallas_call(
        functools.partial(_post_kernel, final_norm),
        grid=(batch, seq // ROW_TILE),
        in_specs=[
            row_spec,
            row_spec,
            _layer_spec(w_o, w_o_layer),
            _mod_spec(mod, layer, 2),
            _const_spec((1, D_MODEL)),
            _mod_spec(mod, layer, 3),
            _mod_spec(mod, layer, 4),
            _mod_spec(mod, layer, 5),
            _layer_spec(w_in, layer),
            _layer_spec(w_out, layer),
            _const_spec((1, D_MODEL)),
        ],
        out_specs=row_spec,
        out_shape=jax.ShapeDtypeStruct((batch, seq, D_MODEL), F32),
        scratch_shapes=[pltpu.VMEM((ROW_TILE, D_FF), BF16)],
        compiler_params=pltpu.CompilerParams(
            dimension_semantics=("arbitrary", "arbitrary"),
            vmem_limit_bytes=VMEM_LIMIT_BYTES),
        name="wo_ffn",
    )(x, o, w_o, mod, norm2_g.reshape(1, D_MODEL), mod, mod, mod, w_in, w_out,
      final_g.reshape(1, D_MODEL))


def _rope_angles(pos, half, theta):
    freqs = theta ** (-jnp.arange(half, dtype=F32) / half)
    ang = pos.astype(F32)[:, None] * freqs[None, :]
    return jnp.cos(ang), jnp.sin(ang)


def _rope_tables(pos_per_lane_group, half, theta, period, seq):
    cos_cols, sf_cols, sb_cols = [], [], []
    for pos in pos_per_lane_group:
        c, s = _rope_angles(pos, half, theta)
        z = jnp.zeros_like(s)
        cos_cols += [c, c]
        sf_cols += [-s, z]
        sb_cols += [z, s]
    rest = period - 2 * half * len(pos_per_lane_group)
    if rest:
        cos_cols.append(jnp.ones((seq, rest), F32))
        sf_cols.append(jnp.zeros((seq, rest), F32))
        sb_cols.append(jnp.zeros((seq, rest), F32))
    reps = LANES // period
    return tuple(jnp.tile(jnp.concatenate(cols, axis=1), (1, reps))
                 for cols in (cos_cols, sf_cols, sb_cols))


def kernel(x, c, ada_w, ada_b, norm1_g, norm2_g, a_w_qkv, a_w_o, a_lam_q1, a_lam_k1, a_lam_q2,
           a_lam_k2, a_subln_g, b_w_qkv, b_w_o, b_qnorm_g, b_knorm_g, f_w_in, f_w_out, final_g):
    batch, seq, _ = x.shape
    assert seq % ROW_TILE == 0 and seq % KV_CHUNK == 0 and seq % Q_LANES == 0

    t = jnp.arange(seq, dtype=jnp.int32)
    row_pos, col_pos = t // GRID_W, t % GRID_W
    tables_a = _rope_tables([t], A_ROT // 2, ROPE_THETA_1D, A_HEAD_DIM, seq)
    tables_b = _rope_tables([row_pos, col_pos], B_HEAD_DIM // 4, ROPE_THETA_AXIAL,
                            B_HEAD_DIM, seq)
    cos_a, sin_a = _rope_angles(t, A_ROT // 2, ROPE_THETA_1D)
    tables_a_t = (cos_a.T, sin_a.T)
    cos_r, sin_r = _rope_angles(row_pos, B_HEAD_DIM // 4, ROPE_THETA_AXIAL)
    cos_c, sin_c = _rope_angles(col_pos, B_HEAD_DIM // 4, ROPE_THETA_AXIAL)
    tables_b_t = (jnp.concatenate([cos_r, cos_r, cos_c, cos_c], axis=1).T,
                  jnp.concatenate([-sin_r, sin_r, -sin_c, sin_c], axis=1).T)

    mod = _mod_call(c, ada_w, ada_b).reshape(DEPTH * batch, 1, 6 * D_MODEL)

    for i in range(DEPTH):
        j = i // 2
        if i % 2 == 0:
            w = a_w_qkv[j].astype(BF16)
            w_qv_t = jnp.concatenate([w[:, :D_MODEL], w[:, 2 * D_MODEL:]], axis=1).T
            q_t, k, v_t = _qkv_call(_qkv_a_kernel, "norm_qkv_rope_a", i, x, mod, norm1_g[i],
                                    w_qv_t, w[:, D_MODEL:2 * D_MODEL], [], tables_a, tables_a_t)
            o = _attn_a_call(i, q_t, k, v_t,
                             (a_lam_q1[j], a_lam_k1[j], a_lam_q2[j], a_lam_k2[j]), a_subln_g[j])
            w_o = a_w_o
        else:
            kv_w = B_N_KV * B_HEAD_DIM
            w = b_w_qkv[j].astype(BF16)
            w_qv_t = jnp.concatenate([w[:, :D_MODEL], w[:, D_MODEL + kv_w:]], axis=1).T
            q_gain = jnp.broadcast_to(b_qnorm_g[j][:, None], (B_HEAD_DIM, ROW_TILE))
            q_t, k, v_t = _qkv_call(_qkv_b_kernel, "norm_qkv_rope_b", i, x, mod, norm1_g[i],
                                    w_qv_t, w[:, D_MODEL:D_MODEL + kv_w],
                                    [q_gain, b_knorm_g[j].reshape(1, LANES)],
                                    tables_b, tables_b_t)
            o = _attn_b_call(q_t, k, v_t)
            w_o = b_w_o
        x = _post_call(i, i == DEPTH - 1, x, o, mod, w_o, j, norm2_g[i], f_w_in, f_w_out,
                       final_g)
    return x
```

```python
import functools
import math

import jax
import jax.numpy as jnp
import numpy as np
from jax.experimental import pallas as pl
from jax.experimental.pallas import tpu as pltpu

D_MODEL = 1024
DEPTH = 2
GRID_W = 64
EPS = 1e-6
LOG2E = math.log2(math.e)

A_HEAD_DIM = 64
A_N_HEADS = D_MODEL // (2 * A_HEAD_DIM)
A_ROT = A_HEAD_DIM // 4
ROPE_THETA_1D = 500000.0

B_HEAD_DIM = 128
B_N_HEADS = D_MODEL // B_HEAD_DIM
B_N_KV = max(1, B_N_HEADS // 4)
B_GROUP = B_N_HEADS // B_N_KV
ROPE_THETA_AXIAL = 10000.0

D_FF = -(-8 * D_MODEL // (3 * 256)) * 256

LANES = 128
SUBLANES = 8
VMEM_LIMIT_BYTES = 56 * 1024 * 1024

ROW_TILE = 512
PROJ_WIDTH = 512
Q_LANES = 1024
Q_TILE_A = Q_LANES // 2
Q_TILE_B = Q_LANES // B_GROUP
KV_CHUNK = 256
FF_CHUNK = 256
ACC_ROWS = LANES + SUBLANES
PIPELINE_LAG = 2

BF16 = jnp.bfloat16
F32 = jnp.float32


def _lambda_init(layer_idx):
    return 0.8 - 0.6 * math.exp(-0.3 * layer_idx)


def _rms_scale(x):
    return jax.lax.rsqrt(jnp.mean(x * x, axis=-1, keepdims=True) + EPS)


def _mod_kernel(c_ref, w_ref, b_ref, o_ref):
    c = c_ref[...]
    cond = c * jax.nn.sigmoid(c)
    y = jnp.dot(cond.astype(BF16), w_ref[0].astype(BF16), preferred_element_type=F32)
    o_ref[0] = y + b_ref[0]


def _mod_call(c, ada_w, ada_b):
    rows = c.shape[0]
    n_chunks = 6
    return pl.pallas_call(
        _mod_kernel,
        grid=(DEPTH, n_chunks),
        in_specs=[
            pl.BlockSpec((rows, D_MODEL), lambda i, j: (0, 0)),
            pl.BlockSpec((1, D_MODEL, D_MODEL), lambda i, j: (i, 0, j)),
            pl.BlockSpec((1, 1, D_MODEL), lambda i, j: (i, 0, j)),
        ],
        out_specs=pl.BlockSpec((1, rows, D_MODEL), lambda i, j: (i, 0, j)),
        out_shape=jax.ShapeDtypeStruct((DEPTH, rows, 6 * D_MODEL), F32),
        compiler_params=pltpu.CompilerParams(
            dimension_semantics=("arbitrary", "arbitrary"),
            vmem_limit_bytes=VMEM_LIMIT_BYTES),
        name="adaln_mod",
    )(c, ada_w, ada_b.reshape(DEPTH, 1, 6 * D_MODEL))


def _rope(x, cos_t, sin_fwd, sin_bwd, shift):
    fwd = pltpu.roll(x, LANES - shift, axis=1)
    bwd = pltpu.roll(x, shift, axis=1)
    return x * cos_t + fwd * sin_fwd + bwd * sin_bwd


def _modulated_norm(x, g, scale, shift):
    y = x * _rms_scale(x)
    return (y * g) * (1.0 + scale) + shift


def _transposed_projection(w_t_ref, h, block):
    n_rows = w_t_ref.shape[0]
    h_t = h.T
    for r in range(0, n_rows, block):
        yield r, jnp.dot(w_t_ref[r:min(r + block, n_rows), :], h_t, preferred_element_type=F32)


def _qkv_a_kernel(x_ref, g_ref, shift_ref, scale_ref, wqv_t_ref, wk_ref, cos_ref, sf_ref, sb_ref,
                  cos_t_ref, sin_t_ref, qt_ref, k_ref, vt_ref):
    h = _modulated_norm(x_ref[0], g_ref[...], scale_ref[0], shift_ref[0]).astype(BF16)

    half = A_ROT // 2
    cos_t, sf, sb = cos_ref[...], sf_ref[...], sb_ref[...]
    for c in range(0, D_MODEL, PROJ_WIDTH):
        y = jnp.dot(h, wk_ref[:, c:c + PROJ_WIDTH], preferred_element_type=F32)
        for u in range(0, PROJ_WIDTH, LANES):
            k = _rope(y[:, u:u + LANES], cos_t, sf, sb, half)
            k_ref[0, :, c + u:c + u + LANES] = k.astype(BF16)

    cos_rows, sin_rows = cos_t_ref[...], sin_t_ref[...]
    for r, y in _transposed_projection(wqv_t_ref, h, PROJ_WIDTH):
        if r < D_MODEL:
            pieces = []
            for base in range(0, PROJ_WIDTH, A_HEAD_DIM):
                lo, hi = y[base:base + half], y[base + half:base + A_ROT]
                pieces += [lo * cos_rows - hi * sin_rows, hi * cos_rows + lo * sin_rows,
                           y[base + A_ROT:base + A_HEAD_DIM]]
            q_t = jnp.concatenate(pieces, axis=0) * (A_HEAD_DIM ** -0.5 * LOG2E)
            qt_ref[0, r:r + PROJ_WIDTH, :] = q_t.astype(BF16)
        else:
            vt_ref[0, r - D_MODEL:r - D_MODEL + PROJ_WIDTH, :] = y.astype(BF16)


def _qkv_b_kernel(x_ref, g_ref, shift_ref, scale_ref, wqv_t_ref, wk_ref, qg_ref, kg_ref,
                  cos_ref, sf_ref, sb_ref, cos_t_ref, sin_t_ref, qt_ref, k_ref, vt_ref):
    h = _modulated_norm(x_ref[0], g_ref[...], scale_ref[0], shift_ref[0]).astype(BF16)

    quarter = B_HEAD_DIM // 4
    cos_t, sf, sb = cos_ref[...], sf_ref[...], sb_ref[...]
    y = jnp.dot(h, wk_ref[...], preferred_element_type=F32)
    for u in range(0, wk_ref.shape[1], LANES):
        k = y[:, u:u + LANES]
        k = (k * _rms_scale(k)) * kg_ref[...]
        k = _rope(k, cos_t, sf, sb, quarter)
        k_ref[0, :, u:u + LANES] = k.astype(BF16)

    cos_rows, sin_rows, q_gain = cos_t_ref[...], sin_t_ref[...], qg_ref[...]
    for r, y in _transposed_projection(wqv_t_ref, h, PROJ_WIDTH):
        if r < D_MODEL:
            for base in range(0, PROJ_WIDTH, B_HEAD_DIM):
                q = y[base:base + B_HEAD_DIM]
                inv = jax.lax.rsqrt(jnp.mean(q * q, axis=0, keepdims=True) + EPS)
                q = (q * inv) * q_gain
                partner = jnp.concatenate(
                    [q[quarter:2 * quarter], q[:quarter], q[3 * quarter:], q[2 * quarter:3 * quarter]],
                    axis=0)
                q = (q * cos_rows + partner * sin_rows) * (B_HEAD_DIM ** -0.5 * LOG2E)
                qt_ref[0, r + base:r + base + B_HEAD_DIM, :] = q.astype(BF16)
        else:
            vt_ref[0, r - D_MODEL:r - D_MODEL + y.shape[0], :] = y.astype(BF16)


def _mod_spec(mod, layer, chunk):
    batch = mod.shape[0] // DEPTH
    return pl.BlockSpec((1, 1, D_MODEL), lambda b, i: (layer * batch + b, 0, chunk))


def _const_spec(shape):
    return pl.BlockSpec(shape, lambda *_: (0,) * len(shape), pipeline_mode=pl.Buffered(1))


def _layer_spec(stacked, layer):
    return pl.BlockSpec((1,) + stacked.shape[1:], lambda *_: (layer, 0, 0),
                        pipeline_mode=pl.Buffered(1))


def _qkv_call(kernel_fn, name, layer, x, mod, norm_g, w_qv_t, w_k, extra, tables, tables_t):
    batch, seq, _ = x.shape
    n_q = D_MODEL
    n_v = w_qv_t.shape[0] - n_q
    n_k = w_k.shape[1]
    table_spec = pl.BlockSpec((ROW_TILE, LANES), lambda b, i: (i, 0))
    table_t_spec = pl.BlockSpec((tables_t[0].shape[0], ROW_TILE), lambda b, i: (0, i))
    in_specs = [
        pl.BlockSpec((1, ROW_TILE, D_MODEL), lambda b, i: (b, i, 0)),
        _const_spec((1, D_MODEL)),
        _mod_spec(mod, layer, 0),
        _mod_spec(mod, layer, 1),
        _const_spec(w_qv_t.shape),
        _const_spec(w_k.shape),
    ] + [_const_spec(e.shape) for e in extra] + [table_spec] * 3 + [table_t_spec] * 2
    return pl.pallas_call(
        kernel_fn,
        grid=(batch, seq // ROW_TILE),
        in_specs=in_specs,
        out_specs=[
            pl.BlockSpec((1, n_q, ROW_TILE), lambda b, i: (b, 0, i)),
            pl.BlockSpec((1, ROW_TILE, n_k), lambda b, i: (b, i, 0)),
            pl.BlockSpec((1, n_v, ROW_TILE), lambda b, i: (b, 0, i)),
        ],
        out_shape=[
            jax.ShapeDtypeStruct((batch, n_q, seq), BF16),
            jax.ShapeDtypeStruct((batch, seq, n_k), BF16),
            jax.ShapeDtypeStruct((batch, n_v, seq), BF16),
        ],
        compiler_params=pltpu.CompilerParams(
            dimension_semantics=("arbitrary", "arbitrary"),
            vmem_limit_bytes=VMEM_LIMIT_BYTES),
        name=name,
    )(x, norm_g.reshape(1, D_MODEL), mod, mod, w_qv_t, w_k, *extra, *tables, *tables_t)


def _scores_and_weighted_sum(q_t, k_ref, vt_ref, write, read, seq):
    s_w_ref, m_w_ref = write
    s_r_ref, m_r_ref = read
    n = q_t.shape[1]
    groups = KV_CHUNK // SUBLANES
    m_prev = m_r_ref[0:1, :]
    col_max = None
    col_sum = None
    acc = None
    for j in range(seq // KV_CHUNK):
        rows = slice(j * KV_CHUNK, (j + 1) * KV_CHUNK)
        s = jnp.dot(k_ref[0, rows, :], q_t, preferred_element_type=F32)
        s_w_ref[rows, :] = s
        part = jnp.max(s.reshape(groups, SUBLANES, n), axis=0)
        col_max = part if col_max is None else jnp.maximum(col_max, part)

        p = jnp.exp2(s_r_ref[rows, :] - m_prev)
        part = jnp.sum(p.reshape(groups, SUBLANES, n), axis=0)
        col_sum = part if col_sum is None else col_sum + part
        pv = jnp.dot(vt_ref[0, :, rows], p.astype(BF16), preferred_element_type=F32)
        acc = pv if acc is None else acc + pv
    m = jnp.max(col_max, axis=0, keepdims=True)
    m_w_ref[...] = jnp.broadcast_to(m, (SUBLANES, n))
    return jnp.concatenate([acc, col_sum], axis=0)


def _pipelined_step(buffers, step_fn):
    t = pl.program_id(0)
    s1_ref, m1_ref, acc1_ref = buffers[1]

    @pl.when(t == 0)
    def _():
        s1_ref[...] = jnp.zeros(s1_ref.shape, F32)
        m1_ref[...] = jnp.zeros(m1_ref.shape, F32)
        acc1_ref[...] = jnp.ones(acc1_ref.shape, F32)

    @pl.when(t % 2 == 0)
    def _():
        step_fn(buffers[0], buffers[1])

    @pl.when(t % 2 == 1)
    def _():
        step_fn(buffers[1], buffers[0])


def _normalised(acc, cols):
    return acc[:LANES, cols] / jnp.sum(acc[LANES:, cols], axis=0, keepdims=True)


def _attn_a_kernel(lam_init, q_ref, k_ref, vt_ref, lq1_ref, lk1_ref, lq2_ref, lk2_ref, g_ref,
                   o_ref, *scratch):
    seq = k_ref.shape[1]
    tq = q_ref.shape[2]

    def step(write, read):
        acc_done = read[2][...]
        lam = (jnp.exp(jnp.sum(lq1_ref[...] * lk1_ref[...], axis=1, keepdims=True))
               - jnp.exp(jnp.sum(lq2_ref[...] * lk2_ref[...], axis=1, keepdims=True))
               + lam_init)
        o_t = (_normalised(acc_done, slice(0, tq))
               - lam * _normalised(acc_done, slice(tq, 2 * tq)))
        o = o_t.T
        o = (o * _rms_scale(o)) * g_ref[...]
        o_ref[0] = (o * (1.0 - lam_init)).astype(BF16)

        q = q_ref[0]
        row = jax.lax.broadcasted_iota(jnp.int32, q.shape, 0)
        zero = jnp.zeros_like(q)
        q_t = jnp.concatenate([jnp.where(row < A_HEAD_DIM, q, zero),
                               jnp.where(row >= A_HEAD_DIM, q, zero)], axis=1)
        write[2][...] = _scores_and_weighted_sum(q_t, k_ref, vt_ref, write[:2], read[:2], seq)

    _pipelined_step((scratch[:3], scratch[3:]), step)


def _attn_b_kernel(q_ref, k_ref, vt_ref, o_ref, *scratch):
    seq = k_ref.shape[1]
    tq = q_ref.shape[2]

    def step(write, read):
        acc_done = read[2][...]
        for g in range(B_GROUP):
            o_t = _normalised(acc_done, slice(g * tq, (g + 1) * tq))
            o_ref[0, :, g * LANES:(g + 1) * LANES] = o_t.T.astype(BF16)

        q = q_ref[0]
        q_t = jnp.concatenate([q[g * LANES:(g + 1) * LANES, :] for g in range(B_GROUP)], axis=1)
        write[2][...] = _scores_and_weighted_sum(q_t, k_ref, vt_ref, write[:2], read[:2], seq)

    _pipelined_step((scratch[:3], scratch[3:]), step)


def _attn_scratch(seq):
    buffer_set = [pltpu.VMEM((seq, Q_LANES), F32),
                  pltpu.VMEM((SUBLANES, Q_LANES), F32),
                  pltpu.VMEM((ACC_ROWS, Q_LANES), F32)]
    return buffer_set + buffer_set


def _attn_call(kernel_fn, name, q_t, k, v_t, n_heads, q_rows, tile, extra):
    batch, seq, _ = k.shape
    tiles_per_head = seq // tile
    n_tiles = batch * n_heads * tiles_per_head

    def split(t):
        t = jnp.clip(t, 0, n_tiles - 1)
        return (t // (n_heads * tiles_per_head), (t // tiles_per_head) % n_heads,
                t % tiles_per_head)

    def q_map(t):
        b, h, i = split(t)
        return (b, h, i)

    def k_map(t):
        b, h, _ = split(t)
        return (b, 0, h)

    def v_map(t):
        b, h, _ = split(t - 1)
        return (b, h, 0)

    def o_map(t):
        b, h, i = split(t - PIPELINE_LAG)
        return (b, i, h)

    return pl.pallas_call(
        kernel_fn,
        grid=(n_tiles + PIPELINE_LAG,),
        in_specs=[pl.BlockSpec((1, q_rows, tile), q_map),
                  pl.BlockSpec((1, seq, LANES), k_map),
                  pl.BlockSpec((1, LANES, seq), v_map)] + [_const_spec(e.shape) for e in extra],
        out_specs=pl.BlockSpec((1, tile, D_MODEL // n_heads), o_map),
        out_shape=jax.ShapeDtypeStruct((batch, seq, D_MODEL), BF16),
        scratch_shapes=_attn_scratch(seq),
        compiler_params=pltpu.CompilerParams(
            dimension_semantics=("arbitrary",),
            vmem_limit_bytes=VMEM_LIMIT_BYTES),
        name=name,
    )(q_t, k, v_t, *extra)


def _attn_a_call(layer, q_t, k, v_t, lam_params, subln_g):
    extra = [p.reshape(1, A_HEAD_DIM) for p in lam_params] + [subln_g.reshape(1, LANES)]
    return _attn_call(functools.partial(_attn_a_kernel, _lambda_init(layer)), "diff_attention",
                      q_t, k, v_t, A_N_HEADS, LANES, Q_TILE_A, extra)


def _attn_b_call(q_t, k, v_t):
    return _attn_call(_attn_b_kernel, "axial_gqa_attention", q_t, k, v_t, B_N_KV,
                      B_GROUP * B_HEAD_DIM, Q_TILE_B, [])


def _post_kernel(final_norm, x_ref, o_ref, wo_ref, gate1_ref, g2_ref, shift2_ref, scale2_ref,
                 gate2_ref, win_ref, wout_ref, fg_ref, out_ref, act_ref):
    y = jnp.dot(o_ref[0], wo_ref[0].astype(BF16), preferred_element_type=F32)
    x1 = x_ref[0] + gate1_ref[0] * y
    out_ref[0] = x1
    h = _modulated_norm(x1, g2_ref[...], scale2_ref[0], shift2_ref[0]).astype(BF16)
    for c in range(D_FF // FF_CHUNK):
        cols = slice(c * FF_CHUNK, (c + 1) * FF_CHUNK)
        up_cols = slice(D_FF + c * FF_CHUNK, D_FF + (c + 1) * FF_CHUNK)
        gate = jnp.dot(h, win_ref[0, :, cols].astype(BF16), preferred_element_type=F32)
        up = jnp.dot(h, win_ref[0, :, up_cols].astype(BF16), preferred_element_type=F32)
        act_ref[:, cols] = ((gate * jax.nn.sigmoid(gate)) * up).astype(BF16)
    f = jnp.dot(act_ref[...], wout_ref[0].astype(BF16), preferred_element_type=F32)
    x2 = out_ref[0] + gate2_ref[0] * f
    if final_norm:
        x2 = (x2 * _rms_scale(x2)) * fg_ref[...]
    out_ref[0] = x2


def _post_call(layer, final_norm, x, o, mod, w_o, w_o_layer, norm2_g, w_in, w_out, final_g):
    batch, seq, _ = x.shape
    row_spec = pl.BlockSpec((1, ROW_TILE, D_MODEL), lambda b, i: (b, i, 0))
    return pl.pallas_call(
        functools.partial(_post_kernel, final_norm),
        grid=(batch, seq // ROW_TILE),
        in_specs=[
            row_spec,
            row_spec,
            _layer_spec(w_o, w_o_layer),
            _mod_spec(mod, layer, 2),
            _const_spec((1, D_MODEL)),
            _mod_spec(mod, layer, 3),
            _mod_spec(mod, layer, 4),
            _mod_spec(mod, layer, 5),
            _layer_spec(w_in, layer),
            _layer_spec(w_out, layer),
            _const_spec((1, D_MODEL)),
        ],
        out_specs=row_spec,
        out_shape=jax.ShapeDtypeStruct((batch, seq, D_MODEL), F32),
        scratch_shapes=[pltpu.VMEM((ROW_TILE, D_FF), BF16)],
        compiler_params=pltpu.CompilerParams(
            dimension_semantics=("arbitrary", "arbitrary"),
            vmem_limit_bytes=VMEM_LIMIT_BYTES),
        name="wo_ffn",
    )(x, o, w_o, mod, norm2_g.reshape(1, D_MODEL), mod, mod, mod, w_in, w_out,
      final_g.reshape(1, D_MODEL))


def _rope_angles(pos, half, theta):
    freqs = float(theta) ** (-np.arange(half, dtype=np.float64) / half)
    ang = np.asarray(pos, np.float64)[:, None] * freqs[None, :]
    return np.cos(ang), np.sin(ang)


def _rope_tables(pos_per_lane_group, half, theta, period, seq):
    cos_cols, sf_cols, sb_cols = [], [], []
    for pos in pos_per_lane_group:
        c, s = _rope_angles(pos, half, theta)
        z = np.zeros_like(s)
        cos_cols += [c, c]
        sf_cols += [-s, z]
        sb_cols += [z, s]
    rest = period - 2 * half * len(pos_per_lane_group)
    if rest:
        cos_cols.append(np.ones((seq, rest)))
        sf_cols.append(np.zeros((seq, rest)))
        sb_cols.append(np.zeros((seq, rest)))
    reps = LANES // period
    return tuple(_table(np.tile(np.concatenate(cols, axis=1), (1, reps)))
                 for cols in (cos_cols, sf_cols, sb_cols))


def _table(values):
    return jnp.asarray(np.ascontiguousarray(values, dtype=np.float32))


def kernel(x, c, ada_w, ada_b, norm1_g, norm2_g, a_w_qkv, a_w_o, a_lam_q1, a_lam_k1, a_lam_q2,
           a_lam_k2, a_subln_g, b_w_qkv, b_w_o, b_qnorm_g, b_knorm_g, f_w_in, f_w_out, final_g):
    batch, seq, _ = x.shape
    assert seq % ROW_TILE == 0 and seq % KV_CHUNK == 0 and seq % Q_LANES == 0

    t = np.arange(seq)
    row_pos, col_pos = t // GRID_W, t % GRID_W
    tables_a = _rope_tables([t], A_ROT // 2, ROPE_THETA_1D, A_HEAD_DIM, seq)
    tables_b = _rope_tables([row_pos, col_pos], B_HEAD_DIM // 4, ROPE_THETA_AXIAL,
                            B_HEAD_DIM, seq)
    cos_a, sin_a = _rope_angles(t, A_ROT // 2, ROPE_THETA_1D)
    tables_a_t = (_table(cos_a.T), _table(sin_a.T))
    cos_r, sin_r = _rope_angles(row_pos, B_HEAD_DIM // 4, ROPE_THETA_AXIAL)
    cos_c, sin_c = _rope_angles(col_pos, B_HEAD_DIM // 4, ROPE_THETA_AXIAL)
    tables_b_t = (_table(np.concatenate([cos_r, cos_r, cos_c, cos_c], axis=1).T),
                  _table(np.concatenate([-sin_r, sin_r, -sin_c, sin_c], axis=1).T))

    mod = _mod_call(c, ada_w, ada_b).reshape(DEPTH * batch, 1, 6 * D_MODEL)

    for i in range(DEPTH):
        j = i // 2
        if i % 2 == 0:
            w = a_w_qkv[j].astype(BF16)
            w_qv_t = jnp.concatenate([w[:, :D_MODEL], w[:, 2 * D_MODEL:]], axis=1).T
            q_t, k, v_t = _qkv_call(_qkv_a_kernel, "norm_qkv_rope_a", i, x, mod, norm1_g[i],
                                    w_qv_t, w[:, D_MODEL:2 * D_MODEL], [], tables_a, tables_a_t)
            o = _attn_a_call(i, q_t, k, v_t,
                             (a_lam_q1[j], a_lam_k1[j], a_lam_q2[j], a_lam_k2[j]), a_subln_g[j])
            w_o = a_w_o
        else:
            kv_w = B_N_KV * B_HEAD_DIM
            w = b_w_qkv[j].astype(BF16)
            w_qv_t = jnp.concatenate([w[:, :D_MODEL], w[:, D_MODEL + kv_w:]], axis=1).T
            q_gain = jnp.broadcast_to(b_qnorm_g[j][:, None], (B_HEAD_DIM, ROW_TILE))
            q_t, k, v_t = _qkv_call(_qkv_b_kernel, "norm_qkv_rope_b", i, x, mod, norm1_g[i],
                                    w_qv_t, w[:, D_MODEL:D_MODEL + kv_w],
                                    [q_gain, b_knorm_g[j].reshape(1, LANES)],
                                    tables_b, tables_b_t)
            o = _attn_b_call(q_t, k, v_t)
            w_o = b_w_o
        x = _post_call(i, i == DEPTH - 1, x, o, mod, w_o, j, norm2_g[i], f_w_in, f_w_out,
                       final_g)
    return x
```

```python
import functools
import math

import jax
import jax.numpy as jnp
import numpy as np
from jax.experimental import pallas as pl
from jax.experimental.pallas import tpu as pltpu

D_MODEL = 1024
DEPTH = 2
GRID_W = 64
EPS = 1e-6
LOG2E = math.log2(math.e)

A_HEAD_DIM = 64
A_N_HEADS = D_MODEL // (2 * A_HEAD_DIM)
A_ROT = A_HEAD_DIM // 4
ROPE_THETA_1D = 500000.0

B_HEAD_DIM = 128
B_N_HEADS = D_MODEL // B_HEAD_DIM
B_N_KV = max(1, B_N_HEADS // 4)
B_GROUP = B_N_HEADS // B_N_KV
ROPE_THETA_AXIAL = 10000.0

D_FF = -(-8 * D_MODEL // (3 * 256)) * 256

LANES = 128
SUBLANES = 8
VMEM_LIMIT_BYTES = 56 * 1024 * 1024

ROW_TILE = 512
QKV_ROW_TILE = 512
PROJ_WIDTH = 512
Q_LANES = 1024
Q_TILE_A = Q_LANES // 2
Q_TILE_B = Q_LANES // B_GROUP
KV_CHUNK = 256
FF_CHUNK = 256
ACC_ROWS = LANES + SUBLANES
PIPELINE_LAG = 2

BF16 = jnp.bfloat16
F32 = jnp.float32


def _lambda_init(layer_idx):
    return 0.8 - 0.6 * math.exp(-0.3 * layer_idx)


def _rms_scale(x):
    return jax.lax.rsqrt(jnp.mean(x * x, axis=-1, keepdims=True) + EPS)


def _mod_kernel(c_ref, w_ref, b_ref, o_ref):
    c = c_ref[...]
    cond = c * jax.nn.sigmoid(c)
    y = jnp.dot(cond.astype(BF16), w_ref[0].astype(BF16), preferred_element_type=F32)
    o_ref[0] = y + b_ref[0]


def _mod_call(c, ada_w, ada_b):
    rows = c.shape[0]
    n_chunks = 6
    return pl.pallas_call(
        _mod_kernel,
        grid=(DEPTH, n_chunks),
        in_specs=[
            pl.BlockSpec((rows, D_MODEL), lambda i, j: (0, 0)),
            pl.BlockSpec((1, D_MODEL, D_MODEL), lambda i, j: (i, 0, j)),
            pl.BlockSpec((1, 1, D_MODEL), lambda i, j: (i, 0, j)),
        ],
        out_specs=pl.BlockSpec((1, rows, D_MODEL), lambda i, j: (i, 0, j)),
        out_shape=jax.ShapeDtypeStruct((DEPTH, rows, 6 * D_MODEL), F32),
        compiler_params=pltpu.CompilerParams(
            dimension_semantics=("arbitrary", "arbitrary"),
            vmem_limit_bytes=VMEM_LIMIT_BYTES),
        name="adaln_mod",
    )(c, ada_w, ada_b.reshape(DEPTH, 1, 6 * D_MODEL))


def _rope(x, cos_t, sin_fwd, sin_bwd, shift):
    fwd = pltpu.roll(x, LANES - shift, axis=1)
    bwd = pltpu.roll(x, shift, axis=1)
    return x * cos_t + fwd * sin_fwd + bwd * sin_bwd


def _modulated_norm(x, g, scale, shift):
    y = x * _rms_scale(x)
    return (y * g) * (1.0 + scale) + shift


def _transposed_projection(w_t_ref, h, block):
    n_rows = w_t_ref.shape[0]
    h_t = h.T
    for r in range(0, n_rows, block):
        yield r, jnp.dot(w_t_ref[r:min(r + block, n_rows), :], h_t, preferred_element_type=F32)


def _qkv_a_kernel(x_ref, g_ref, shift_ref, scale_ref, wqv_t_ref, wk_ref, cos_ref, sf_ref, sb_ref,
                  cos_t_ref, sin_t_ref, qt_ref, k_ref, vt_ref):
    h = _modulated_norm(x_ref[0], g_ref[...], scale_ref[0], shift_ref[0]).astype(BF16)

    half = A_ROT // 2
    cos_t, sf, sb = cos_ref[...], sf_ref[...], sb_ref[...]
    for c in range(0, D_MODEL, PROJ_WIDTH):
        y = jnp.dot(h, wk_ref[:, c:c + PROJ_WIDTH], preferred_element_type=F32)
        for u in range(0, PROJ_WIDTH, LANES):
            k = _rope(y[:, u:u + LANES], cos_t, sf, sb, half)
            k_ref[0, :, c + u:c + u + LANES] = k.astype(BF16)

    cos_rows, sin_rows = cos_t_ref[...], sin_t_ref[...]
    for r, y in _transposed_projection(wqv_t_ref, h, PROJ_WIDTH):
        if r < D_MODEL:
            pieces = []
            for base in range(0, PROJ_WIDTH, A_HEAD_DIM):
                lo, hi = y[base:base + half], y[base + half:base + A_ROT]
                pieces += [lo * cos_rows - hi * sin_rows, hi * cos_rows + lo * sin_rows,
                           y[base + A_ROT:base + A_HEAD_DIM]]
            q_t = jnp.concatenate(pieces, axis=0) * (A_HEAD_DIM ** -0.5 * LOG2E)
            qt_ref[0, r:r + PROJ_WIDTH, :] = q_t.astype(BF16)
        else:
            vt_ref[0, r - D_MODEL:r - D_MODEL + PROJ_WIDTH, :] = y.astype(BF16)


def _qkv_b_kernel(x_ref, g_ref, shift_ref, scale_ref, wqv_t_ref, wk_ref, qg_ref, kg_ref,
                  cos_ref, sf_ref, sb_ref, cos_t_ref, sin_t_ref, qt_ref, k_ref, vt_ref):
    h = _modulated_norm(x_ref[0], g_ref[...], scale_ref[0], shift_ref[0]).astype(BF16)

    quarter = B_HEAD_DIM // 4
    cos_t, sf, sb = cos_ref[...], sf_ref[...], sb_ref[...]
    y = jnp.dot(h, wk_ref[...], preferred_element_type=F32)
    for u in range(0, wk_ref.shape[1], LANES):
        k = y[:, u:u + LANES]
        k = (k * _rms_scale(k)) * kg_ref[...]
        k = _rope(k, cos_t, sf, sb, quarter)
        k_ref[0, :, u:u + LANES] = k.astype(BF16)

    cos_rows, sin_rows, q_gain = cos_t_ref[...], sin_t_ref[...], qg_ref[...]
    for r, y in _transposed_projection(wqv_t_ref, h, PROJ_WIDTH):
        if r < D_MODEL:
            for base in range(0, PROJ_WIDTH, B_HEAD_DIM):
                q = y[base:base + B_HEAD_DIM]
                inv = jax.lax.rsqrt(jnp.mean(q * q, axis=0, keepdims=True) + EPS)
                q = (q * inv) * q_gain
                partner = jnp.concatenate(
                    [q[quarter:2 * quarter], q[:quarter], q[3 * quarter:], q[2 * quarter:3 * quarter]],
                    axis=0)
                q = (q * cos_rows + partner * sin_rows) * (B_HEAD_DIM ** -0.5 * LOG2E)
                qt_ref[0, r + base:r + base + B_HEAD_DIM, :] = q.astype(BF16)
        else:
            vt_ref[0, r - D_MODEL:r - D_MODEL + y.shape[0], :] = y.astype(BF16)


def _mod_spec(mod, layer, chunk):
    batch = mod.shape[0] // DEPTH
    return pl.BlockSpec((1, 1, D_MODEL), lambda b, i: (layer * batch + b, 0, chunk))


def _const_spec(shape):
    return pl.BlockSpec(shape, lambda *_: (0,) * len(shape), pipeline_mode=pl.Buffered(1))


def _layer_spec(stacked, layer):
    return pl.BlockSpec((1,) + stacked.shape[1:], lambda *_: (layer, 0, 0),
                        pipeline_mode=pl.Buffered(1))


def _qkv_call(kernel_fn, name, layer, x, mod, norm_g, w_qv_t, w_k, extra, tables, tables_t):
    batch, seq, _ = x.shape
    n_q = D_MODEL
    n_v = w_qv_t.shape[0] - n_q
    n_k = w_k.shape[1]
    table_spec = pl.BlockSpec((QKV_ROW_TILE, LANES), lambda b, i: (i, 0))
    table_t_spec = pl.BlockSpec((tables_t[0].shape[0], QKV_ROW_TILE), lambda b, i: (0, i))
    in_specs = [
        pl.BlockSpec((1, QKV_ROW_TILE, D_MODEL), lambda b, i: (b, i, 0)),
        _const_spec((1, D_MODEL)),
        _mod_spec(mod, layer, 0),
        _mod_spec(mod, layer, 1),
        _const_spec(w_qv_t.shape),
        _const_spec(w_k.shape),
    ] + [_const_spec(e.shape) for e in extra] + [table_spec] * 3 + [table_t_spec] * 2
    return pl.pallas_call(
        kernel_fn,
        grid=(batch, seq // QKV_ROW_TILE),
        in_specs=in_specs,
        out_specs=[
            pl.BlockSpec((1, n_q, QKV_ROW_TILE), lambda b, i: (b, 0, i)),
            pl.BlockSpec((1, QKV_ROW_TILE, n_k), lambda b, i: (b, i, 0)),
            pl.BlockSpec((1, n_v, QKV_ROW_TILE), lambda b, i: (b, 0, i)),
        ],
        out_shape=[
            jax.ShapeDtypeStruct((batch, n_q, seq), BF16),
            jax.ShapeDtypeStruct((batch, seq, n_k), BF16),
            jax.ShapeDtypeStruct((batch, n_v, seq), BF16),
        ],
        compiler_params=pltpu.CompilerParams(
            dimension_semantics=("arbitrary", "arbitrary"),
            vmem_limit_bytes=VMEM_LIMIT_BYTES),
        name=name,
    )(x, norm_g.reshape(1, D_MODEL), mod, mod, w_qv_t, w_k, *extra, *tables, *tables_t)


def _scores_and_weighted_sum(q_t, k_ref, vt_ref, write, read, seq):
    s_w_ref, m_w_ref = write
    s_r_ref, m_r_ref = read
    n = q_t.shape[1]
    groups = KV_CHUNK // SUBLANES
    m_prev = m_r_ref[0:1, :]
    col_max = None
    col_sum = None
    acc = None
    for j in range(seq // KV_CHUNK):
        rows = slice(j * KV_CHUNK, (j + 1) * KV_CHUNK)
        p = jnp.exp2(s_r_ref[rows, :] - m_prev)
        part = jnp.sum(p.reshape(groups, SUBLANES, n), axis=0)
        col_sum = part if col_sum is None else col_sum + part
        pv = jnp.dot(vt_ref[0, :, rows], p.astype(BF16), preferred_element_type=F32)
        acc = pv if acc is None else acc + pv

        s = jnp.dot(k_ref[0, rows, :], q_t, preferred_element_type=F32)
        s_w_ref[rows, :] = s
        part = jnp.max(s.reshape(groups, SUBLANES, n), axis=0)
        col_max = part if col_max is None else jnp.maximum(col_max, part)
    m = jnp.max(col_max, axis=0, keepdims=True)
    m_w_ref[...] = jnp.broadcast_to(m, (SUBLANES, n))
    return jnp.concatenate([acc, col_sum], axis=0)


def _pipelined_step(buffers, step_fn):
    t = pl.program_id(0)
    s1_ref, m1_ref, acc1_ref = buffers[1]

    @pl.when(t == 0)
    def _():
        s1_ref[...] = jnp.zeros(s1_ref.shape, F32)
        m1_ref[...] = jnp.zeros(m1_ref.shape, F32)
        acc1_ref[...] = jnp.ones(acc1_ref.shape, F32)

    @pl.when(t % 2 == 0)
    def _():
        step_fn(buffers[0], buffers[1])

    @pl.when(t % 2 == 1)
    def _():
        step_fn(buffers[1], buffers[0])


def _normalised(acc, cols):
    return acc[:LANES, cols] / jnp.sum(acc[LANES:, cols], axis=0, keepdims=True)


def _attn_a_kernel(lam_init, q_ref, k_ref, vt_ref, lq1_ref, lk1_ref, lq2_ref, lk2_ref, g_ref,
                   o_ref, *scratch):
    seq = k_ref.shape[1]
    tq = q_ref.shape[2]

    def step(write, read):
        acc_done = read[2][...]
        lam = (jnp.exp(jnp.sum(lq1_ref[...] * lk1_ref[...], axis=1, keepdims=True))
               - jnp.exp(jnp.sum(lq2_ref[...] * lk2_ref[...], axis=1, keepdims=True))
               + lam_init)
        o_t = (_normalised(acc_done, slice(0, tq))
               - lam * _normalised(acc_done, slice(tq, 2 * tq)))
        o = o_t.T
        o = (o * _rms_scale(o)) * g_ref[...]
        o_ref[0] = (o * (1.0 - lam_init)).astype(BF16)

        q = q_ref[0]
        row = jax.lax.broadcasted_iota(jnp.int32, q.shape, 0)
        zero = jnp.zeros_like(q)
        q_t = jnp.concatenate([jnp.where(row < A_HEAD_DIM, q, zero),
                               jnp.where(row >= A_HEAD_DIM, q, zero)], axis=1)
        write[2][...] = _scores_and_weighted_sum(q_t, k_ref, vt_ref, write[:2], read[:2], seq)

    _pipelined_step((scratch[:3], scratch[3:]), step)


def _attn_b_kernel(q_ref, k_ref, vt_ref, o_ref, *scratch):
    seq = k_ref.shape[1]
    tq = q_ref.shape[2]

    def step(write, read):
        acc_done = read[2][...]
        for g in range(B_GROUP):
            o_t = _normalised(acc_done, slice(g * tq, (g + 1) * tq))
            o_ref[0, :, g * LANES:(g + 1) * LANES] = o_t.T.astype(BF16)

        q = q_ref[0]
        q_t = jnp.concatenate([q[g * LANES:(g + 1) * LANES, :] for g in range(B_GROUP)], axis=1)
        write[2][...] = _scores_and_weighted_sum(q_t, k_ref, vt_ref, write[:2], read[:2], seq)

    _pipelined_step((scratch[:3], scratch[3:]), step)


def _attn_scratch(seq):
    buffer_set = [pltpu.VMEM((seq, Q_LANES), F32),
                  pltpu.VMEM((SUBLANES, Q_LANES), F32),
                  pltpu.VMEM((ACC_ROWS, Q_LANES), F32)]
    return buffer_set + buffer_set


def _attn_call(kernel_fn, name, q_t, k, v_t, n_heads, q_rows, tile, extra):
    batch, seq, _ = k.shape
    tiles_per_head = seq // tile
    n_tiles = batch * n_heads * tiles_per_head

    def split(t):
        t = jnp.clip(t, 0, n_tiles - 1)
        return (t // (n_heads * tiles_per_head), (t // tiles_per_head) % n_heads,
                t % tiles_per_head)

    def q_map(t):
        b, h, i = split(t)
        return (b, h, i)

    def k_map(t):
        b, h, _ = split(t)
        return (b, 0, h)

    def v_map(t):
        b, h, _ = split(t - 1)
        return (b, h, 0)

    def o_map(t):
        b, h, i = split(t - PIPELINE_LAG)
        return (b, i, h)

    return pl.pallas_call(
        kernel_fn,
        grid=(n_tiles + PIPELINE_LAG,),
        in_specs=[pl.BlockSpec((1, q_rows, tile), q_map),
                  pl.BlockSpec((1, seq, LANES), k_map),
                  pl.BlockSpec((1, LANES, seq), v_map)] + [_const_spec(e.shape) for e in extra],
        out_specs=pl.BlockSpec((1, tile, D_MODEL // n_heads), o_map),
        out_shape=jax.ShapeDtypeStruct((batch, seq, D_MODEL), BF16),
        scratch_shapes=_attn_scratch(seq),
        compiler_params=pltpu.CompilerParams(
            dimension_semantics=("arbitrary",),
            vmem_limit_bytes=VMEM_LIMIT_BYTES),
        name=name,
    )(q_t, k, v_t, *extra)


def _attn_a_call(layer, q_t, k, v_t, lam_params, subln_g):
    extra = [p.reshape(1, A_HEAD_DIM) for p in lam_params] + [subln_g.reshape(1, LANES)]
    return _attn_call(functools.partial(_attn_a_kernel, _lambda_init(layer)), "diff_attention",
                      q_t, k, v_t, A_N_HEADS, LANES, Q_TILE_A, extra)


def _attn_b_call(q_t, k, v_t):
    return _attn_call(_attn_b_kernel, "axial_gqa_attention", q_t, k, v_t, B_N_KV,
                      B_GROUP * B_HEAD_DIM, Q_TILE_B, [])


def _post_kernel(final_norm, x_ref, o_ref, wo_ref, gate1_ref, g2_ref, shift2_ref, scale2_ref,
                 gate2_ref, win_ref, wout_ref, fg_ref, out_ref, act_ref):
    y = jnp.dot(o_ref[0], wo_ref[0].astype(BF16), preferred_element_type=F32)
    x1 = x_ref[0] + gate1_ref[0] * y
    out_ref[0] = x1
    h = _modulated_norm(x1, g2_ref[...], scale2_ref[0], shift2_ref[0]).astype(BF16)
    for c in range(D_FF // FF_CHUNK):
        cols = slice(c * FF_CHUNK, (c + 1) * FF_CHUNK)
        up_cols = slice(D_FF + c * FF_CHUNK, D_FF + (c + 1) * FF_CHUNK)
        gate = jnp.dot(h, win_ref[0, :, cols].astype(BF16), preferred_element_type=F32)
        up = jnp.dot(h, win_ref[0, :, up_cols].astype(BF16), preferred_element_type=F32)
        act_ref[:, cols] = ((gate * jax.nn.sigmoid(gate)) * up).astype(BF16)
    f = jnp.dot(act_ref[...], wout_ref[0].astype(BF16), preferred_element_type=F32)
    x2 = out_ref[0] + gate2_ref[0] * f
    if final_norm:
        x2 = (x2 * _rms_scale(x2)) * fg_ref[...]
    out_ref[0] = x2


def _post_call(layer, final_norm, x, o, mod, w_o, w_o_layer, norm2_g, w_in, w_out, final_g):
    batch, seq, _ = x.shape
    row_spec = pl.BlockSpec((1, ROW_TILE, D_MODEL), lambda b, i: (b, i, 0))
    return pl.pallas_call(
        functools.partial(_post_kernel, final_norm),
        grid=(batch, seq // ROW_TILE),
        in_specs=[
            row_spec,
            row_spec,
            _layer_spec(w_o, w_o_layer),
            _mod_spec(mod, layer, 2),
            _const_spec((1, D_MODEL)),
            _mod_spec(mod, layer, 3),
            _mod_spec(mod, layer, 4),
            _mod_spec(mod, layer, 5),
            _layer_spec(w_in, layer),
            _layer_spec(w_out, layer),
            _const_spec((1, D_MODEL)),
        ],
        out_specs=row_spec,
        out_shape=jax.ShapeDtypeStruct((batch, seq, D_MODEL), F32),
        scratch_shapes=[pltpu.VMEM((ROW_TILE, D_FF), BF16)],
        compiler_params=pltpu.CompilerParams(
            dimension_semantics=("arbitrary", "arbitrary"),
            vmem_limit_bytes=VMEM_LIMIT_BYTES),
        name="wo_ffn",
    )(x, o, w_o, mod, norm2_g.reshape(1, D_MODEL), mod, mod, mod, w_in, w_out,
      final_g.reshape(1, D_MODEL))


def _rope_angles(pos, half, theta):
    freqs = float(theta) ** (-np.arange(half, dtype=np.float64) / half)
    ang = np.asarray(pos, np.float64)[:, None] * freqs[None, :]
    return np.cos(ang), np.sin(ang)


def _rope_tables(pos_per_lane_group, half, theta, period, seq):
    cos_cols, sf_cols, sb_cols = [], [], []
    for pos in pos_per_lane_group:
        c, s = _rope_angles(pos, half, theta)
        z = np.zeros_like(s)
        cos_cols += [c, c]
        sf_cols += [-s, z]
        sb_cols += [z, s]
    rest = period - 2 * half * len(pos_per_lane_group)
    if rest:
        cos_cols.append(np.ones((seq, rest)))
        sf_cols.append(np.zeros((seq, rest)))
        sb_cols.append(np.zeros((seq, rest)))
    reps = LANES // period
    return tuple(_table(np.tile(np.concatenate(cols, axis=1), (1, reps)))
                 for cols in (cos_cols, sf_cols, sb_cols))


def _table(values):
    return jnp.asarray(np.ascontiguousarray(values, dtype=np.float32))


def kernel(x, c, ada_w, ada_b, norm1_g, norm2_g, a_w_qkv, a_w_o, a_lam_q1, a_lam_k1, a_lam_q2,
           a_lam_k2, a_subln_g, b_w_qkv, b_w_o, b_qnorm_g, b_knorm_g, f_w_in, f_w_out, final_g):
    batch, seq, _ = x.shape
    assert seq % QKV_ROW_TILE == 0 and seq % ROW_TILE == 0
    assert seq % KV_CHUNK == 0 and seq % Q_LANES == 0

    t = np.arange(seq)
    row_pos, col_pos = t // GRID_W, t % GRID_W
    tables_a = _rope_tables([t], A_ROT // 2, ROPE_THETA_1D, A_HEAD_DIM, seq)
    tables_b = _rope_tables([row_pos, col_pos], B_HEAD_DIM // 4, ROPE_THETA_AXIAL,
                            B_HEAD_DIM, seq)
    cos_a, sin_a = _rope_angles(t, A_ROT // 2, ROPE_THETA_1D)
    tables_a_t = (_table(cos_a.T), _table(sin_a.T))
    cos_r, sin_r = _rope_angles(row_pos, B_HEAD_DIM // 4, ROPE_THETA_AXIAL)
    cos_c, sin_c = _rope_angles(col_pos, B_HEAD_DIM // 4, ROPE_THETA_AXIAL)
    tables_b_t = (_table(np.concatenate([cos_r, cos_r, cos_c, cos_c], axis=1).T),
                  _table(np.concatenate([-sin_r, sin_r, -sin_c, sin_c], axis=1).T))

    mod = _mod_call(c, ada_w, ada_b).reshape(DEPTH * batch, 1, 6 * D_MODEL)

    for i in range(DEPTH):
        j = i // 2
        if i % 2 == 0:
            w = a_w_qkv[j].astype(BF16)
            w_qv_t = jnp.concatenate([w[:, :D_MODEL], w[:, 2 * D_MODEL:]], axis=1).T
            q_t, k, v_t = _qkv_call(_qkv_a_kernel, "norm_qkv_rope_a", i, x, mod, norm1_g[i],
                                    w_qv_t, w[:, D_MODEL:2 * D_MODEL], [], tables_a, tables_a_t)
            o = _attn_a_call(i, q_t, k, v_t,
                             (a_lam_q1[j], a_lam_k1[j], a_lam_q2[j], a_lam_k2[j]), a_subln_g[j])
            w_o = a_w_o
        else:
            kv_w = B_N_KV * B_HEAD_DIM
            w = b_w_qkv[j].astype(BF16)
            w_qv_t = jnp.concatenate([w[:, :D_MODEL], w[:, D_MODEL + kv_w:]], axis=1).T
            q_gain = jnp.broadcast_to(b_qnorm_g[j][:, None], (B_HEAD_DIM, QKV_ROW_TILE))
            q_t, k, v_t = _qkv_call(_qkv_b_kernel, "norm_qkv_rope_b", i, x, mod, norm1_g[i],
                                    w_qv_t, w[:, D_MODEL:D_MODEL + kv_w],
                                    [q_gain, b_knorm_g[j].reshape(1, LANES)],
                                    tables_b, tables_b_t)
            o = _attn_b_call(q_t, k, v_t)
            w_o = b_w_o
        x = _post_call(i, i == DEPTH - 1, x, o, mod, w_o, j, norm2_g[i], f_w_in, f_w_out,
                       final_g)
    return x
```

```python
import functools
import math

import jax
import jax.numpy as jnp
import numpy as np
from jax.experimental import pallas as pl
from jax.experimental.pallas import tpu as pltpu

D_MODEL = 1024
DEPTH = 2
GRID_W = 64
EPS = 1e-6
LOG2E = math.log2(math.e)

A_HEAD_DIM = 64
A_N_HEADS = D_MODEL // (2 * A_HEAD_DIM)
A_ROT = A_HEAD_DIM // 4
ROPE_THETA_1D = 500000.0

B_HEAD_DIM = 128
B_N_HEADS = D_MODEL // B_HEAD_DIM
B_N_KV = max(1, B_N_HEADS // 4)
B_GROUP = B_N_HEADS // B_N_KV
ROPE_THETA_AXIAL = 10000.0

D_FF = -(-8 * D_MODEL // (3 * 256)) * 256

LANES = 128
SUBLANES = 8
VMEM_LIMIT_BYTES = 56 * 1024 * 1024

ROW_TILE = 512
QKV_ROW_TILE = 512
PROJ_WIDTH = 1024
Q_LANES = 1024
Q_TILE_A = Q_LANES // 2
Q_TILE_B = Q_LANES // B_GROUP
KV_CHUNK = 256
FF_CHUNK = 256
ACC_ROWS = LANES + SUBLANES
PIPELINE_LAG = 2

BF16 = jnp.bfloat16
F32 = jnp.float32


def _lambda_init(layer_idx):
    return 0.8 - 0.6 * math.exp(-0.3 * layer_idx)


def _rms_scale(x):
    return jax.lax.rsqrt(jnp.mean(x * x, axis=-1, keepdims=True) + EPS)


def _mod_kernel(c_ref, w_ref, b_ref, o_ref):
    c = c_ref[...]
    cond = c * jax.nn.sigmoid(c)
    y = jnp.dot(cond.astype(BF16), w_ref[0].astype(BF16), preferred_element_type=F32)
    o_ref[0] = y + b_ref[0]


def _mod_call(c, ada_w, ada_b):
    rows = c.shape[0]
    n_chunks = 6
    return pl.pallas_call(
        _mod_kernel,
        grid=(DEPTH, n_chunks),
        in_specs=[
            pl.BlockSpec((rows, D_MODEL), lambda i, j: (0, 0)),
            pl.BlockSpec((1, D_MODEL, D_MODEL), lambda i, j: (i, 0, j)),
            pl.BlockSpec((1, 1, D_MODEL), lambda i, j: (i, 0, j)),
        ],
        out_specs=pl.BlockSpec((1, rows, D_MODEL), lambda i, j: (i, 0, j)),
        out_shape=jax.ShapeDtypeStruct((DEPTH, rows, 6 * D_MODEL), F32),
        compiler_params=pltpu.CompilerParams(
            dimension_semantics=("arbitrary", "arbitrary"),
            vmem_limit_bytes=VMEM_LIMIT_BYTES),
        name="adaln_mod",
    )(c, ada_w, ada_b.reshape(DEPTH, 1, 6 * D_MODEL))


def _rope(x, cos_t, sin_fwd, sin_bwd, shift):
    fwd = pltpu.roll(x, LANES - shift, axis=1)
    bwd = pltpu.roll(x, shift, axis=1)
    return x * cos_t + fwd * sin_fwd + bwd * sin_bwd


def _modulated_norm(x, g, scale, shift):
    return (x * _rms_scale(x)) * (g * (1.0 + scale)) + shift


def _transposed_projection(w_t_ref, h, block):
    n_rows = w_t_ref.shape[0]
    h_t = h.T
    for r in range(0, n_rows, block):
        yield r, jnp.dot(w_t_ref[r:min(r + block, n_rows), :], h_t, preferred_element_type=F32)


def _qkv_a_kernel(x_ref, g_ref, shift_ref, scale_ref, wqv_t_ref, wk_ref, cos_ref, sf_ref, sb_ref,
                  cos_t_ref, sin_t_ref, qt_ref, k_ref, vt_ref):
    h = _modulated_norm(x_ref[0], g_ref[...], scale_ref[0], shift_ref[0]).astype(BF16)

    half = A_ROT // 2
    cos_t, sf, sb = cos_ref[...], sf_ref[...], sb_ref[...]
    for c in range(0, D_MODEL, PROJ_WIDTH):
        y = jnp.dot(h, wk_ref[:, c:c + PROJ_WIDTH], preferred_element_type=F32)
        for u in range(0, PROJ_WIDTH, LANES):
            k = _rope(y[:, u:u + LANES], cos_t, sf, sb, half)
            k_ref[0, :, c + u:c + u + LANES] = k.astype(BF16)

    cos_rows, sin_rows = cos_t_ref[...], sin_t_ref[...]
    for r, y in _transposed_projection(wqv_t_ref, h, PROJ_WIDTH):
        if r < D_MODEL:
            pieces = []
            for base in range(0, PROJ_WIDTH, A_HEAD_DIM):
                lo, hi = y[base:base + half], y[base + half:base + A_ROT]
                pieces += [lo * cos_rows - hi * sin_rows, hi * cos_rows + lo * sin_rows,
                           y[base + A_ROT:base + A_HEAD_DIM]]
            q_t = jnp.concatenate(pieces, axis=0) * (A_HEAD_DIM ** -0.5 * LOG2E)
            qt_ref[0, r:r + PROJ_WIDTH, :] = q_t.astype(BF16)
        else:
            vt_ref[0, r - D_MODEL:r - D_MODEL + PROJ_WIDTH, :] = y.astype(BF16)


def _qkv_b_kernel(x_ref, g_ref, shift_ref, scale_ref, wqv_t_ref, wk_ref, qg_ref, kg_ref,
                  cos_ref, sf_ref, sb_ref, cos_t_ref, sin_t_ref, qt_ref, k_ref, vt_ref):
    h = _modulated_norm(x_ref[0], g_ref[...], scale_ref[0], shift_ref[0]).astype(BF16)

    quarter = B_HEAD_DIM // 4
    cos_t, sf, sb = cos_ref[...], sf_ref[...], sb_ref[...]
    y = jnp.dot(h, wk_ref[...], preferred_element_type=F32)
    for u in range(0, wk_ref.shape[1], LANES):
        k = y[:, u:u + LANES]
        k = (k * _rms_scale(k)) * kg_ref[...]
        k = _rope(k, cos_t, sf, sb, quarter)
        k_ref[0, :, u:u + LANES] = k.astype(BF16)

    def swap_pairs(a):
        return jnp.concatenate(
            [a[quarter:2 * quarter], a[:quarter], a[3 * quarter:], a[2 * quarter:3 * quarter]],
            axis=0)

    q_gain = qg_ref[...]
    gain_cos = q_gain * cos_t_ref[...]
    gain_sin = swap_pairs(q_gain) * sin_t_ref[...]
    for r, y in _transposed_projection(wqv_t_ref, h, PROJ_WIDTH):
        if r < D_MODEL:
            for base in range(0, PROJ_WIDTH, B_HEAD_DIM):
                q = y[base:base + B_HEAD_DIM]
                q = q * jax.lax.rsqrt(jnp.mean(q * q, axis=0, keepdims=True) + EPS)
                q = q * gain_cos + swap_pairs(q) * gain_sin
                qt_ref[0, r + base:r + base + B_HEAD_DIM, :] = q.astype(BF16)
        else:
            vt_ref[0, r - D_MODEL:r - D_MODEL + y.shape[0], :] = y.astype(BF16)


def _mod_spec(mod, layer, chunk):
    batch = mod.shape[0] // DEPTH
    return pl.BlockSpec((1, 1, D_MODEL), lambda b, i: (layer * batch + b, 0, chunk))


def _const_spec(shape):
    return pl.BlockSpec(shape, lambda *_: (0,) * len(shape), pipeline_mode=pl.Buffered(1))


def _layer_spec(stacked, layer):
    return pl.BlockSpec((1,) + stacked.shape[1:], lambda *_: (layer, 0, 0),
                        pipeline_mode=pl.Buffered(1))


def _qkv_call(kernel_fn, name, layer, x, mod, norm_g, w_qv_t, w_k, extra, tables, tables_t):
    batch, seq, _ = x.shape
    n_q = D_MODEL
    n_v = w_qv_t.shape[0] - n_q
    n_k = w_k.shape[1]
    table_spec = pl.BlockSpec((QKV_ROW_TILE, LANES), lambda b, i: (i, 0))
    table_t_spec = pl.BlockSpec((tables_t[0].shape[0], QKV_ROW_TILE), lambda b, i: (0, i))
    in_specs = [
        pl.BlockSpec((1, QKV_ROW_TILE, D_MODEL), lambda b, i: (b, i, 0)),
        _const_spec((1, D_MODEL)),
        _mod_spec(mod, layer, 0),
        _mod_spec(mod, layer, 1),
        _const_spec(w_qv_t.shape),
        _const_spec(w_k.shape),
    ] + [_const_spec(e.shape) for e in extra] + [table_spec] * 3 + [table_t_spec] * 2
    return pl.pallas_call(
        kernel_fn,
        grid=(batch, seq // QKV_ROW_TILE),
        in_specs=in_specs,
        out_specs=[
            pl.BlockSpec((1, n_q, QKV_ROW_TILE), lambda b, i: (b, 0, i)),
            pl.BlockSpec((1, QKV_ROW_TILE, n_k), lambda b, i: (b, i, 0)),
            pl.BlockSpec((1, n_v, QKV_ROW_TILE), lambda b, i: (b, 0, i)),
        ],
        out_shape=[
            jax.ShapeDtypeStruct((batch, n_q, seq), BF16),
            jax.ShapeDtypeStruct((batch, seq, n_k), BF16),
            jax.ShapeDtypeStruct((batch, n_v, seq), BF16),
        ],
        compiler_params=pltpu.CompilerParams(
            dimension_semantics=("arbitrary", "arbitrary"),
            vmem_limit_bytes=VMEM_LIMIT_BYTES),
        name=name,
    )(x, norm_g.reshape(1, D_MODEL), mod, mod, w_qv_t, w_k, *extra, *tables, *tables_t)


def _scores_and_weighted_sum(q_t, k_ref, vt_ref, write, read, seq):
    s_w_ref, m_w_ref = write
    s_r_ref, m_r_ref = read
    n = q_t.shape[1]
    groups = KV_CHUNK // SUBLANES
    m_prev = m_r_ref[0:1, :]
    col_max = None
    col_sum = None
    acc = None
    for j in range(seq // KV_CHUNK):
        rows = slice(j * KV_CHUNK, (j + 1) * KV_CHUNK)
        p = jnp.exp2(s_r_ref[rows, :] - m_prev)
        part = jnp.sum(p.reshape(groups, SUBLANES, n), axis=0)
        col_sum = part if col_sum is None else col_sum + part
        pv = jnp.dot(vt_ref[0, :, rows], p.astype(BF16), preferred_element_type=F32)
        acc = pv if acc is None else acc + pv

        s = jnp.dot(k_ref[0, rows, :], q_t, preferred_element_type=F32)
        s_w_ref[rows, :] = s
        part = jnp.max(s.reshape(groups, SUBLANES, n), axis=0)
        col_max = part if col_max is None else jnp.maximum(col_max, part)
    m = jnp.max(col_max, axis=0, keepdims=True)
    m_w_ref[...] = jnp.broadcast_to(m, (SUBLANES, n))
    return jnp.concatenate([acc, col_sum], axis=0)


def _pipelined_step(buffers, step_fn):
    t = pl.program_id(0)
    s1_ref, m1_ref, acc1_ref = buffers[1]

    @pl.when(t == 0)
    def _():
        s1_ref[...] = jnp.zeros(s1_ref.shape, F32)
        m1_ref[...] = jnp.zeros(m1_ref.shape, F32)
        acc1_ref[...] = jnp.ones(acc1_ref.shape, F32)

    @pl.when(t % 2 == 0)
    def _():
        step_fn(buffers[0], buffers[1])

    @pl.when(t % 2 == 1)
    def _():
        step_fn(buffers[1], buffers[0])


def _normalised(acc, cols):
    return acc[:LANES, cols] / jnp.sum(acc[LANES:, cols], axis=0, keepdims=True)


def _attn_a_kernel(lam_init, q_ref, k_ref, vt_ref, lq1_ref, lk1_ref, lq2_ref, lk2_ref, g_ref,
                   o_ref, *scratch):
    seq = k_ref.shape[1]
    tq = q_ref.shape[2]

    def step(write, read):
        acc_done = read[2][...]
        lam = (jnp.exp(jnp.sum(lq1_ref[...] * lk1_ref[...], axis=1, keepdims=True))
               - jnp.exp(jnp.sum(lq2_ref[...] * lk2_ref[...], axis=1, keepdims=True))
               + lam_init)
        o_t = (_normalised(acc_done, slice(0, tq))
               - lam * _normalised(acc_done, slice(tq, 2 * tq)))
        o = o_t.T
        o = (o * _rms_scale(o)) * g_ref[...]
        o_ref[0] = (o * (1.0 - lam_init)).astype(BF16)

        q = q_ref[0]
        row = jax.lax.broadcasted_iota(jnp.int32, q.shape, 0)
        zero = jnp.zeros_like(q)
        q_t = jnp.concatenate([jnp.where(row < A_HEAD_DIM, q, zero),
                               jnp.where(row >= A_HEAD_DIM, q, zero)], axis=1)
        write[2][...] = _scores_and_weighted_sum(q_t, k_ref, vt_ref, write[:2], read[:2], seq)

    _pipelined_step((scratch[:3], scratch[3:]), step)


def _attn_b_kernel(q_ref, k_ref, vt_ref, o_ref, *scratch):
    seq = k_ref.shape[1]
    tq = q_ref.shape[2]

    def step(write, read):
        acc_done = read[2][...]
        for g in range(B_GROUP):
            o_t = _normalised(acc_done, slice(g * tq, (g + 1) * tq))
            o_ref[0, :, g * LANES:(g + 1) * LANES] = o_t.T.astype(BF16)

        q = q_ref[0]
        q_t = jnp.concatenate([q[g * LANES:(g + 1) * LANES, :] for g in range(B_GROUP)], axis=1)
        write[2][...] = _scores_and_weighted_sum(q_t, k_ref, vt_ref, write[:2], read[:2], seq)

    _pipelined_step((scratch[:3], scratch[3:]), step)


def _attn_scratch(seq):
    buffer_set = [pltpu.VMEM((seq, Q_LANES), F32),
                  pltpu.VMEM((SUBLANES, Q_LANES), F32),
                  pltpu.VMEM((ACC_ROWS, Q_LANES), F32)]
    return buffer_set + buffer_set


def _attn_call(kernel_fn, name, q_t, k, v_t, n_heads, q_rows, tile, extra):
    batch, seq, _ = k.shape
    tiles_per_head = seq // tile
    n_tiles = batch * n_heads * tiles_per_head

    def split(t):
        t = jnp.clip(t, 0, n_tiles - 1)
        return (t // (n_heads * tiles_per_head), (t // tiles_per_head) % n_heads,
                t % tiles_per_head)

    def q_map(t):
        b, h, i = split(t)
        return (b, h, i)

    def k_map(t):
        b, h, _ = split(t)
        return (b, 0, h)

    def v_map(t):
        b, h, _ = split(t - 1)
        return (b, h, 0)

    def o_map(t):
        b, h, i = split(t - PIPELINE_LAG)
        return (b, i, h)

    return pl.pallas_call(
        kernel_fn,
        grid=(n_tiles + PIPELINE_LAG,),
        in_specs=[pl.BlockSpec((1, q_rows, tile), q_map),
                  pl.BlockSpec((1, seq, LANES), k_map),
                  pl.BlockSpec((1, LANES, seq), v_map)] + [_const_spec(e.shape) for e in extra],
        out_specs=pl.BlockSpec((1, tile, D_MODEL // n_heads), o_map),
        out_shape=jax.ShapeDtypeStruct((batch, seq, D_MODEL), BF16),
        scratch_shapes=_attn_scratch(seq),
        compiler_params=pltpu.CompilerParams(
            dimension_semantics=("arbitrary",),
            vmem_limit_bytes=VMEM_LIMIT_BYTES),
        name=name,
    )(q_t, k, v_t, *extra)


def _attn_a_call(layer, q_t, k, v_t, lam_params, subln_g):
    extra = [p.reshape(1, A_HEAD_DIM) for p in lam_params] + [subln_g.reshape(1, LANES)]
    return _attn_call(functools.partial(_attn_a_kernel, _lambda_init(layer)), "diff_attention",
                      q_t, k, v_t, A_N_HEADS, LANES, Q_TILE_A, extra)


def _attn_b_call(q_t, k, v_t):
    return _attn_call(_attn_b_kernel, "axial_gqa_attention", q_t, k, v_t, B_N_KV,
                      B_GROUP * B_HEAD_DIM, Q_TILE_B, [])


def _post_kernel(final_norm, x_ref, o_ref, wo_ref, gate1_ref, g2_ref, shift2_ref, scale2_ref,
                 gate2_ref, win_ref, wout_ref, fg_ref, out_ref, act_ref):
    y = jnp.dot(o_ref[0], wo_ref[0].astype(BF16), preferred_element_type=F32)
    x1 = x_ref[0] + gate1_ref[0] * y
    out_ref[0] = x1
    h = _modulated_norm(x1, g2_ref[...], scale2_ref[0], shift2_ref[0]).astype(BF16)
    for c in range(D_FF // FF_CHUNK):
        cols = slice(c * FF_CHUNK, (c + 1) * FF_CHUNK)
        up_cols = slice(D_FF + c * FF_CHUNK, D_FF + (c + 1) * FF_CHUNK)
        gate = jnp.dot(h, win_ref[0, :, cols].astype(BF16), preferred_element_type=F32)
        up = jnp.dot(h, win_ref[0, :, up_cols].astype(BF16), preferred_element_type=F32)
        act_ref[:, cols] = ((gate * jax.nn.sigmoid(gate)) * up).astype(BF16)
    f = jnp.dot(act_ref[...], wout_ref[0].astype(BF16), preferred_element_type=F32)
    x2 = out_ref[0] + gate2_ref[0] * f
    if final_norm:
        x2 = (x2 * _rms_scale(x2)) * fg_ref[...]
    out_ref[0] = x2


def _post_call(layer, final_norm, x, o, mod, w_o, w_o_layer, norm2_g, w_in, w_out, final_g):
    batch, seq, _ = x.shape
    row_spec = pl.BlockSpec((1, ROW_TILE, D_MODEL), lambda b, i: (b, i, 0))
    return pl.pallas_call(
        functools.partial(_post_kernel, final_norm),
        grid=(batch, seq // ROW_TILE),
        in_specs=[
            row_spec,
            row_spec,
            _layer_spec(w_o, w_o_layer),
            _mod_spec(mod, layer, 2),
            _const_spec((1, D_MODEL)),
            _mod_spec(mod, layer, 3),
            _mod_spec(mod, layer, 4),
            _mod_spec(mod, layer, 5),
            _layer_spec(w_in, layer),
            _layer_spec(w_out, layer),
            _const_spec((1, D_MODEL)),
        ],
        out_specs=row_spec,
        out_shape=jax.ShapeDtypeStruct((batch, seq, D_MODEL), F32),
        scratch_shapes=[pltpu.VMEM((ROW_TILE, D_FF), BF16)],
        compiler_params=pltpu.CompilerParams(
            dimension_semantics=("arbitrary", "arbitrary"),
            vmem_limit_bytes=VMEM_LIMIT_BYTES),
        name="wo_ffn",
    )(x, o, w_o, mod, norm2_g.reshape(1, D_MODEL), mod, mod, mod, w_in, w_out,
      final_g.reshape(1, D_MODEL))


def _rope_angles(pos, half, theta):
    freqs = float(theta) ** (-np.arange(half, dtype=np.float64) / half)
    ang = np.asarray(pos, np.float64)[:, None] * freqs[None, :]
    return np.cos(ang), np.sin(ang)


def _rope_tables(pos_per_lane_group, half, theta, period, seq):
    cos_cols, sf_cols, sb_cols = [], [], []
    for pos in pos_per_lane_group:
        c, s = _rope_angles(pos, half, theta)
        z = np.zeros_like(s)
        cos_cols += [c, c]
        sf_cols += [-s, z]
        sb_cols += [z, s]
    rest = period - 2 * half * len(pos_per_lane_group)
    if rest:
        cos_cols.append(np.ones((seq, rest)))
        sf_cols.append(np.zeros((seq, rest)))
        sb_cols.append(np.zeros((seq, rest)))
    reps = LANES // period
    return tuple(_table(np.tile(np.concatenate(cols, axis=1), (1, reps)))
                 for cols in (cos_cols, sf_cols, sb_cols))


def _table(values):
    return jnp.asarray(np.ascontiguousarray(values, dtype=np.float32))


def kernel(x, c, ada_w, ada_b, norm1_g, norm2_g, a_w_qkv, a_w_o, a_lam_q1, a_lam_k1, a_lam_q2,
           a_lam_k2, a_subln_g, b_w_qkv, b_w_o, b_qnorm_g, b_knorm_g, f_w_in, f_w_out, final_g):
    batch, seq, _ = x.shape
    assert seq % QKV_ROW_TILE == 0 and seq % ROW_TILE == 0
    assert seq % KV_CHUNK == 0 and seq % Q_LANES == 0

    t = np.arange(seq)
    row_pos, col_pos = t // GRID_W, t % GRID_W
    tables_a = _rope_tables([t], A_ROT // 2, ROPE_THETA_1D, A_HEAD_DIM, seq)
    tables_b = _rope_tables([row_pos, col_pos], B_HEAD_DIM // 4, ROPE_THETA_AXIAL,
                            B_HEAD_DIM, seq)
    cos_a, sin_a = _rope_angles(t, A_ROT // 2, ROPE_THETA_1D)
    tables_a_t = (_table(cos_a.T), _table(sin_a.T))
    cos_r, sin_r = _rope_angles(row_pos, B_HEAD_DIM // 4, ROPE_THETA_AXIAL)
    cos_c, sin_c = _rope_angles(col_pos, B_HEAD_DIM // 4, ROPE_THETA_AXIAL)
    q_scale_b = B_HEAD_DIM ** -0.5 * LOG2E
    tables_b_t = (_table(q_scale_b * np.concatenate([cos_r, cos_r, cos_c, cos_c], axis=1).T),
                  _table(q_scale_b * np.concatenate([-sin_r, sin_r, -sin_c, sin_c], axis=1).T))

    mod = _mod_call(c, ada_w, ada_b).reshape(DEPTH * batch, 1, 6 * D_MODEL)

    for i in range(DEPTH):
        j = i // 2
        if i % 2 == 0:
            w = a_w_qkv[j].astype(BF16)
            w_qv_t = jnp.concatenate([w[:, :D_MODEL], w[:, 2 * D_MODEL:]], axis=1).T
            q_t, k, v_t = _qkv_call(_qkv_a_kernel, "norm_qkv_rope_a", i, x, mod, norm1_g[i],
                                    w_qv_t, w[:, D_MODEL:2 * D_MODEL], [], tables_a, tables_a_t)
            o = _attn_a_call(i, q_t, k, v_t,
                             (a_lam_q1[j], a_lam_k1[j], a_lam_q2[j], a_lam_k2[j]), a_subln_g[j])
            w_o = a_w_o
        else:
            kv_w = B_N_KV * B_HEAD_DIM
            w = b_w_qkv[j].astype(BF16)
            w_qv_t = jnp.concatenate([w[:, :D_MODEL], w[:, D_MODEL + kv_w:]], axis=1).T
            q_gain = jnp.broadcast_to(b_qnorm_g[j][:, None], (B_HEAD_DIM, QKV_ROW_TILE))
            q_t, k, v_t = _qkv_call(_qkv_b_kernel, "norm_qkv_rope_b", i, x, mod, norm1_g[i],
                                    w_qv_t, w[:, D_MODEL:D_MODEL + kv_w],
                                    [q_gain, b_knorm_g[j].reshape(1, LANES)],
                                    tables_b, tables_b_t)
            o = _attn_b_call(q_t, k, v_t)
            w_o = b_w_o
        x = _post_call(i, i == DEPTH - 1, x, o, mod, w_o, j, norm2_g[i], f_w_in, f_w_out,
                       final_g)
    return x
```

```python
import functools
import math

import jax
import jax.numpy as jnp
import numpy as np
from jax.experimental import pallas as pl
from jax.experimental.pallas import tpu as pltpu

D_MODEL = 1024
DEPTH = 2
GRID_W = 64
EPS = 1e-6
LOG2E = math.log2(math.e)

A_HEAD_DIM = 64
A_N_HEADS = D_MODEL // (2 * A_HEAD_DIM)
A_ROT = A_HEAD_DIM // 4
ROPE_THETA_1D = 500000.0

B_HEAD_DIM = 128
B_N_HEADS = D_MODEL // B_HEAD_DIM
B_N_KV = max(1, B_N_HEADS // 4)
B_GROUP = B_N_HEADS // B_N_KV
ROPE_THETA_AXIAL = 10000.0

D_FF = -(-8 * D_MODEL // (3 * 256)) * 256

LANES = 128
SUBLANES = 8
VMEM_LIMIT_BYTES = 56 * 1024 * 1024

ROW_TILE = 512
QKV_ROW_TILE = 512
PROJ_WIDTH = 1024
Q_LANES = 1024
Q_TILE_A = Q_LANES // 2
Q_TILE_B = Q_LANES // B_GROUP
KV_CHUNK = 256
FF_CHUNK = 256
ACC_ROWS = LANES + SUBLANES
PIPELINE_LAG = 2

BF16 = jnp.bfloat16
F32 = jnp.float32


def _lambda_init(layer_idx):
    return 0.8 - 0.6 * math.exp(-0.3 * layer_idx)


def _rms_scale(x):
    return jax.lax.rsqrt(jnp.mean(x * x, axis=-1, keepdims=True) + EPS)


def _mod_kernel(c_ref, w_ref, b_ref, o_ref):
    c = c_ref[...]
    cond = c * jax.nn.sigmoid(c)
    y = jnp.dot(cond.astype(BF16), w_ref[0].astype(BF16), preferred_element_type=F32)
    o_ref[0] = y + b_ref[0]


def _mod_call(c, ada_w, ada_b):
    rows = c.shape[0]
    n_chunks = 6
    return pl.pallas_call(
        _mod_kernel,
        grid=(DEPTH, n_chunks),
        in_specs=[
            pl.BlockSpec((rows, D_MODEL), lambda i, j: (0, 0)),
            pl.BlockSpec((1, D_MODEL, D_MODEL), lambda i, j: (i, 0, j)),
            pl.BlockSpec((1, 1, D_MODEL), lambda i, j: (i, 0, j)),
        ],
        out_specs=pl.BlockSpec((1, rows, D_MODEL), lambda i, j: (i, 0, j)),
        out_shape=jax.ShapeDtypeStruct((DEPTH, rows, 6 * D_MODEL), F32),
        compiler_params=pltpu.CompilerParams(
            dimension_semantics=("arbitrary", "arbitrary"),
            vmem_limit_bytes=VMEM_LIMIT_BYTES),
        name="adaln_mod",
    )(c, ada_w, ada_b.reshape(DEPTH, 1, 6 * D_MODEL))


def _rope(x, cos_t, sin_fwd, sin_bwd, shift):
    fwd = pltpu.roll(x, LANES - shift, axis=1)
    bwd = pltpu.roll(x, shift, axis=1)
    return x * cos_t + fwd * sin_fwd + bwd * sin_bwd


def _modulated_norm(x, g, scale, shift):
    return (x * _rms_scale(x)) * (g * (1.0 + scale)) + shift


def _transposed_projection(w_t_ref, h, block):
    n_rows = w_t_ref.shape[0]
    h_t = h.T
    for r in range(0, n_rows, block):
        yield r, jnp.dot(w_t_ref[r:min(r + block, n_rows), :], h_t, preferred_element_type=F32)


def _qkv_a_kernel(x_ref, g_ref, shift_ref, scale_ref, wqv_t_ref, wk_ref, cos_ref, sf_ref, sb_ref,
                  cos_t_ref, sin_t_ref, qt_ref, k_ref, vt_ref):
    h = _modulated_norm(x_ref[0], g_ref[...], scale_ref[0], shift_ref[0]).astype(BF16)

    half = A_ROT // 2
    cos_t, sf, sb = cos_ref[...], sf_ref[...], sb_ref[...]
    for c in range(0, D_MODEL, PROJ_WIDTH):
        y = jnp.dot(h, wk_ref[:, c:c + PROJ_WIDTH], preferred_element_type=F32)
        for u in range(0, PROJ_WIDTH, LANES):
            k = _rope(y[:, u:u + LANES], cos_t, sf, sb, half)
            k_ref[0, :, c + u:c + u + LANES] = k.astype(BF16)

    cos_rows, sin_rows = cos_t_ref[...], sin_t_ref[...]
    for r, y in _transposed_projection(wqv_t_ref, h, PROJ_WIDTH):
        if r < D_MODEL:
            pieces = []
            for base in range(0, PROJ_WIDTH, A_HEAD_DIM):
                lo, hi = y[base:base + half], y[base + half:base + A_ROT]
                pieces += [lo * cos_rows - hi * sin_rows, hi * cos_rows + lo * sin_rows,
                           y[base + A_ROT:base + A_HEAD_DIM]]
            q_t = jnp.concatenate(pieces, axis=0) * (A_HEAD_DIM ** -0.5 * LOG2E)
            qt_ref[0, r:r + PROJ_WIDTH, :] = q_t.astype(BF16)
        else:
            vt_ref[0, r - D_MODEL:r - D_MODEL + PROJ_WIDTH, :] = y.astype(BF16)


def _qkv_b_kernel(x_ref, g_ref, shift_ref, scale_ref, wqv_t_ref, wk_ref, qg_ref, kg_ref,
                  cos_ref, sf_ref, sb_ref, cos_t_ref, sin_t_ref, qt_ref, k_ref, vt_ref):
    h = _modulated_norm(x_ref[0], g_ref[...], scale_ref[0], shift_ref[0]).astype(BF16)

    quarter = B_HEAD_DIM // 4
    cos_t, sf, sb = cos_ref[...], sf_ref[...], sb_ref[...]
    y = jnp.dot(h, wk_ref[...], preferred_element_type=F32)
    for u in range(0, wk_ref.shape[1], LANES):
        k = y[:, u:u + LANES]
        k = (k * _rms_scale(k)) * kg_ref[...]
        k = _rope(k, cos_t, sf, sb, quarter)
        k_ref[0, :, u:u + LANES] = k.astype(BF16)

    def swap_pairs(a):
        return jnp.concatenate(
            [a[quarter:2 * quarter], a[:quarter], a[3 * quarter:], a[2 * quarter:3 * quarter]],
            axis=0)

    q_gain = qg_ref[...]
    gain_cos = q_gain * cos_t_ref[...]
    gain_sin = swap_pairs(q_gain) * sin_t_ref[...]
    for r, y in _transposed_projection(wqv_t_ref, h, PROJ_WIDTH):
        if r < D_MODEL:
            for base in range(0, PROJ_WIDTH, B_HEAD_DIM):
                q = y[base:base + B_HEAD_DIM]
                q = q * jax.lax.rsqrt(jnp.mean(q * q, axis=0, keepdims=True) + EPS)
                q = q * gain_cos + swap_pairs(q) * gain_sin
                qt_ref[0, r + base:r + base + B_HEAD_DIM, :] = q.astype(BF16)
        else:
            vt_ref[0, r - D_MODEL:r - D_MODEL + y.shape[0], :] = y.astype(BF16)


def _mod_spec(mod, layer, chunk):
    batch = mod.shape[0] // DEPTH
    return pl.BlockSpec((1, 1, D_MODEL), lambda b, i: (layer * batch + b, 0, chunk))


def _const_spec(shape):
    return pl.BlockSpec(shape, lambda *_: (0,) * len(shape), pipeline_mode=pl.Buffered(1))


def _qkv_call(kernel_fn, name, layer, x, mod, norm_g, w_qv_t, w_k, extra, tables, tables_t):
    batch, seq, _ = x.shape
    n_q = D_MODEL
    n_v = w_qv_t.shape[0] - n_q
    n_k = w_k.shape[1]
    table_spec = pl.BlockSpec((QKV_ROW_TILE, LANES), lambda b, i: (i, 0))
    table_t_spec = pl.BlockSpec((tables_t[0].shape[0], QKV_ROW_TILE), lambda b, i: (0, i))
    in_specs = [
        pl.BlockSpec((1, QKV_ROW_TILE, D_MODEL), lambda b, i: (b, i, 0)),
        _const_spec((1, D_MODEL)),
        _mod_spec(mod, layer, 0),
        _mod_spec(mod, layer, 1),
        _const_spec(w_qv_t.shape),
        _const_spec(w_k.shape),
    ] + [_const_spec(e.shape) for e in extra] + [table_spec] * 3 + [table_t_spec] * 2
    return pl.pallas_call(
        kernel_fn,
        grid=(batch, seq // QKV_ROW_TILE),
        in_specs=in_specs,
        out_specs=[
            pl.BlockSpec((1, n_q, QKV_ROW_TILE), lambda b, i: (b, 0, i)),
            pl.BlockSpec((1, QKV_ROW_TILE, n_k), lambda b, i: (b, i, 0)),
            pl.BlockSpec((1, n_v, QKV_ROW_TILE), lambda b, i: (b, 0, i)),
        ],
        out_shape=[
            jax.ShapeDtypeStruct((batch, n_q, seq), BF16),
            jax.ShapeDtypeStruct((batch, seq, n_k), BF16),
            jax.ShapeDtypeStruct((batch, n_v, seq), BF16),
        ],
        compiler_params=pltpu.CompilerParams(
            dimension_semantics=("arbitrary", "arbitrary"),
            vmem_limit_bytes=VMEM_LIMIT_BYTES),
        name=name,
    )(x, norm_g.reshape(1, D_MODEL), mod, mod, w_qv_t, w_k, *extra, *tables, *tables_t)


def _scores_and_weighted_sum(q_t, k_ref, vt_ref, write, read, seq):
    s_w_ref, m_w_ref = write
    s_r_ref, m_r_ref = read
    n = q_t.shape[1]
    groups = KV_CHUNK // SUBLANES
    m_prev = m_r_ref[0:1, :]
    col_max = None
    col_sum = None
    acc = None
    for j in range(seq // KV_CHUNK):
        rows = slice(j * KV_CHUNK, (j + 1) * KV_CHUNK)
        p = jnp.exp2(s_r_ref[rows, :] - m_prev)
        part = jnp.sum(p.reshape(groups, SUBLANES, n), axis=0)
        col_sum = part if col_sum is None else col_sum + part
        pv = jnp.dot(vt_ref[0, :, rows], p.astype(BF16), preferred_element_type=F32)
        acc = pv if acc is None else acc + pv

        s = jnp.dot(k_ref[0, rows, :], q_t, preferred_element_type=F32)
        s_w_ref[rows, :] = s
        part = jnp.max(s.reshape(groups, SUBLANES, n), axis=0)
        col_max = part if col_max is None else jnp.maximum(col_max, part)
    m = jnp.max(col_max, axis=0, keepdims=True)
    m_w_ref[...] = jnp.broadcast_to(m, (SUBLANES, n))
    return jnp.concatenate([acc, col_sum], axis=0)


def _pipelined_step(buffers, step_fn):
    t = pl.program_id(0)
    s1_ref, m1_ref, acc1_ref = buffers[1]

    @pl.when(t == 0)
    def _():
        s1_ref[...] = jnp.zeros(s1_ref.shape, F32)
        m1_ref[...] = jnp.zeros(m1_ref.shape, F32)
        acc1_ref[...] = jnp.ones(acc1_ref.shape, F32)

    @pl.when(t % 2 == 0)
    def _():
        step_fn(buffers[0], buffers[1])

    @pl.when(t % 2 == 1)
    def _():
        step_fn(buffers[1], buffers[0])


def _normalised(acc, cols):
    return acc[:LANES, cols] / jnp.sum(acc[LANES:, cols], axis=0, keepdims=True)


def _attn_a_kernel(lam_init, q_ref, k_ref, vt_ref, lq1_ref, lk1_ref, lq2_ref, lk2_ref, g_ref,
                   o_ref, *scratch):
    seq = k_ref.shape[1]
    tq = q_ref.shape[2]

    def step(write, read):
        acc_done = read[2][...]
        lam = (jnp.exp(jnp.sum(lq1_ref[...] * lk1_ref[...], axis=1, keepdims=True))
               - jnp.exp(jnp.sum(lq2_ref[...] * lk2_ref[...], axis=1, keepdims=True))
               + lam_init)
        o_t = (_normalised(acc_done, slice(0, tq))
               - lam * _normalised(acc_done, slice(tq, 2 * tq)))
        o = o_t.T
        o = (o * _rms_scale(o)) * g_ref[...]
        o_ref[0] = (o * (1.0 - lam_init)).astype(BF16)

        q = q_ref[0]
        row = jax.lax.broadcasted_iota(jnp.int32, q.shape, 0)
        zero = jnp.zeros_like(q)
        q_t = jnp.concatenate([jnp.where(row < A_HEAD_DIM, q, zero),
                               jnp.where(row >= A_HEAD_DIM, q, zero)], axis=1)
        write[2][...] = _scores_and_weighted_sum(q_t, k_ref, vt_ref, write[:2], read[:2], seq)

    _pipelined_step((scratch[:3], scratch[3:]), step)


def _attn_b_kernel(q_ref, k_ref, vt_ref, o_ref, *scratch):
    seq = k_ref.shape[1]
    tq = q_ref.shape[2]

    def step(write, read):
        acc_done = read[2][...]
        for g in range(B_GROUP):
            o_t = _normalised(acc_done, slice(g * tq, (g + 1) * tq))
            o_ref[0, :, g * LANES:(g + 1) * LANES] = o_t.T.astype(BF16)

        q = q_ref[0]
        q_t = jnp.concatenate([q[g * LANES:(g + 1) * LANES, :] for g in range(B_GROUP)], axis=1)
        write[2][...] = _scores_and_weighted_sum(q_t, k_ref, vt_ref, write[:2], read[:2], seq)

    _pipelined_step((scratch[:3], scratch[3:]), step)


def _attn_scratch(seq):
    buffer_set = [pltpu.VMEM((seq, Q_LANES), F32),
                  pltpu.VMEM((SUBLANES, Q_LANES), F32),
                  pltpu.VMEM((ACC_ROWS, Q_LANES), F32)]
    return buffer_set + buffer_set


def _attn_call(kernel_fn, name, q_t, k, v_t, n_heads, q_rows, tile, extra):
    batch, seq, _ = k.shape
    tiles_per_head = seq // tile
    n_tiles = batch * n_heads * tiles_per_head

    def split(t):
        t = jnp.clip(t, 0, n_tiles - 1)
        return (t // (n_heads * tiles_per_head), (t // tiles_per_head) % n_heads,
                t % tiles_per_head)

    def q_map(t):
        b, h, i = split(t)
        return (b, h, i)

    def k_map(t):
        b, h, _ = split(t)
        return (b, 0, h)

    def v_map(t):
        b, h, _ = split(t - 1)
        return (b, h, 0)

    def o_map(t):
        b, h, i = split(t - PIPELINE_LAG)
        return (b, i, h)

    return pl.pallas_call(
        kernel_fn,
        grid=(n_tiles + PIPELINE_LAG,),
        in_specs=[pl.BlockSpec((1, q_rows, tile), q_map),
                  pl.BlockSpec((1, seq, LANES), k_map),
                  pl.BlockSpec((1, LANES, seq), v_map)] + [_const_spec(e.shape) for e in extra],
        out_specs=pl.BlockSpec((1, tile, D_MODEL // n_heads), o_map),
        out_shape=jax.ShapeDtypeStruct((batch, seq, D_MODEL), BF16),
        scratch_shapes=_attn_scratch(seq),
        compiler_params=pltpu.CompilerParams(
            dimension_semantics=("arbitrary",),
            vmem_limit_bytes=VMEM_LIMIT_BYTES),
        name=name,
    )(q_t, k, v_t, *extra)


def _attn_a_call(layer, q_t, k, v_t, lam_params, subln_g):
    extra = [p.reshape(1, A_HEAD_DIM) for p in lam_params] + [subln_g.reshape(1, LANES)]
    return _attn_call(functools.partial(_attn_a_kernel, _lambda_init(layer)), "diff_attention",
                      q_t, k, v_t, A_N_HEADS, LANES, Q_TILE_A, extra)


def _attn_b_call(q_t, k, v_t):
    return _attn_call(_attn_b_kernel, "axial_gqa_attention", q_t, k, v_t, B_N_KV,
                      B_GROUP * B_HEAD_DIM, Q_TILE_B, [])


def _post_kernel(final_norm, layer, w_o_layer, x_ref, o_ref, wo_hbm, gate1_ref, g2_ref,
                 shift2_ref, scale2_ref, gate2_ref, win_hbm, wout_hbm, fg_ref, out_ref,
                 act_ref, wo_ref, win_ref, wout_ref, sems):
    n_chunks = D_FF // FF_CHUNK

    def cols(c, up):
        start = (D_FF if up else 0) + c * FF_CHUNK
        return slice(start, start + FF_CHUNK)

    pieces = [(wo_hbm.at[w_o_layer], wo_ref)]
    for c in range(n_chunks):
        for up in (False, True):
            pieces.append((win_hbm.at[layer, :, cols(c, up)], win_ref.at[:, cols(c, up)]))
    pieces.append((wout_hbm.at[layer], wout_ref))
    copies = [pltpu.make_async_copy(src, dst, sems.at[i]) for i, (src, dst) in enumerate(pieces)]

    def step(arrived):
        arrived(0)
        y = jnp.dot(o_ref[0], wo_ref[...].astype(BF16), preferred_element_type=F32)
        x1 = x_ref[0] + gate1_ref[0] * y
        out_ref[0] = x1
        h = _modulated_norm(x1, g2_ref[...], scale2_ref[0], shift2_ref[0]).astype(BF16)
        for c in range(n_chunks):
            arrived(1 + 2 * c)
            arrived(2 + 2 * c)
            gate = jnp.dot(h, win_ref[:, cols(c, False)].astype(BF16), preferred_element_type=F32)
            up = jnp.dot(h, win_ref[:, cols(c, True)].astype(BF16), preferred_element_type=F32)
            act_ref[:, cols(c, False)] = ((gate * jax.nn.sigmoid(gate)) * up).astype(BF16)
        arrived(len(copies) - 1)
        f = jnp.dot(act_ref[...], wout_ref[...].astype(BF16), preferred_element_type=F32)
        x2 = out_ref[0] + gate2_ref[0] * f
        if final_norm:
            x2 = (x2 * _rms_scale(x2)) * fg_ref[...]
        out_ref[0] = x2

    first = jnp.logical_and(pl.program_id(0) == 0, pl.program_id(1) == 0)

    @pl.when(first)
    def _():
        for copy in copies:
            copy.start()
        step(lambda i: copies[i].wait())

    @pl.when(jnp.logical_not(first))
    def _():
        step(lambda i: None)


def _post_call(layer, final_norm, x, o, mod, w_o, w_o_layer, norm2_g, w_in, w_out, final_g):
    batch, seq, _ = x.shape
    row_spec = pl.BlockSpec((1, ROW_TILE, D_MODEL), lambda b, i: (b, i, 0))
    hbm_spec = pl.BlockSpec(memory_space=pl.ANY)
    return pl.pallas_call(
        functools.partial(_post_kernel, final_norm, layer, w_o_layer),
        grid=(batch, seq // ROW_TILE),
        in_specs=[
            row_spec,
            row_spec,
            hbm_spec,
            _mod_spec(mod, layer, 2),
            _const_spec((1, D_MODEL)),
            _mod_spec(mod, layer, 3),
            _mod_spec(mod, layer, 4),
            _mod_spec(mod, layer, 5),
            hbm_spec,
            hbm_spec,
            _const_spec((1, D_MODEL)),
        ],
        out_specs=row_spec,
        out_shape=jax.ShapeDtypeStruct((batch, seq, D_MODEL), F32),
        scratch_shapes=[pltpu.VMEM((ROW_TILE, D_FF), BF16),
                        pltpu.VMEM(w_o.shape[1:], w_o.dtype),
                        pltpu.VMEM(w_in.shape[1:], w_in.dtype),
                        pltpu.VMEM(w_out.shape[1:], w_out.dtype),
                        pltpu.SemaphoreType.DMA((2 * (D_FF // FF_CHUNK) + 2,))],
        compiler_params=pltpu.CompilerParams(
            dimension_semantics=("arbitrary", "arbitrary"),
            vmem_limit_bytes=VMEM_LIMIT_BYTES),
        name="wo_ffn",
    )(x, o, w_o, mod, norm2_g.reshape(1, D_MODEL), mod, mod, mod, w_in, w_out,
      final_g.reshape(1, D_MODEL))


def _rope_angles(pos, half, theta):
    freqs = float(theta) ** (-np.arange(half, dtype=np.float64) / half)
    ang = np.asarray(pos, np.float64)[:, None] * freqs[None, :]
    return np.cos(ang), np.sin(ang)


def _rope_tables(pos_per_lane_group, half, theta, period, seq):
    cos_cols, sf_cols, sb_cols = [], [], []
    for pos in pos_per_lane_group:
        c, s = _rope_angles(pos, half, theta)
        z = np.zeros_like(s)
        cos_cols += [c, c]
        sf_cols += [-s, z]
        sb_cols += [z, s]
    rest = period - 2 * half * len(pos_per_lane_group)
    if rest:
        cos_cols.append(np.ones((seq, rest)))
        sf_cols.append(np.zeros((seq, rest)))
        sb_cols.append(np.zeros((seq, rest)))
    reps = LANES // period
    return tuple(_table(np.tile(np.concatenate(cols, axis=1), (1, reps)))
                 for cols in (cos_cols, sf_cols, sb_cols))


def _table(values):
    return jnp.asarray(np.ascontiguousarray(values, dtype=np.float32))


def kernel(x, c, ada_w, ada_b, norm1_g, norm2_g, a_w_qkv, a_w_o, a_lam_q1, a_lam_k1, a_lam_q2,
           a_lam_k2, a_subln_g, b_w_qkv, b_w_o, b_qnorm_g, b_knorm_g, f_w_in, f_w_out, final_g):
    batch, seq, _ = x.shape
    assert seq % QKV_ROW_TILE == 0 and seq % ROW_TILE == 0
    assert seq % KV_CHUNK == 0 and seq % Q_LANES == 0

    t = np.arange(seq)
    row_pos, col_pos = t // GRID_W, t % GRID_W
    tables_a = _rope_tables([t], A_ROT // 2, ROPE_THETA_1D, A_HEAD_DIM, seq)
    tables_b = _rope_tables([row_pos, col_pos], B_HEAD_DIM // 4, ROPE_THETA_AXIAL,
                            B_HEAD_DIM, seq)
    cos_a, sin_a = _rope_angles(t, A_ROT // 2, ROPE_THETA_1D)
    tables_a_t = (_table(cos_a.T), _table(sin_a.T))
    cos_r, sin_r = _rope_angles(row_pos, B_HEAD_DIM // 4, ROPE_THETA_AXIAL)
    cos_c, sin_c = _rope_angles(col_pos, B_HEAD_DIM // 4, ROPE_THETA_AXIAL)
    q_scale_b = B_HEAD_DIM ** -0.5 * LOG2E
    tables_b_t = (_table(q_scale_b * np.concatenate([cos_r, cos_r, cos_c, cos_c], axis=1).T),
                  _table(q_scale_b * np.concatenate([-sin_r, sin_r, -sin_c, sin_c], axis=1).T))

    mod = _mod_call(c, ada_w, ada_b).reshape(DEPTH * batch, 1, 6 * D_MODEL)

    for i in range(DEPTH):
        j = i // 2
        if i % 2 == 0:
            w = a_w_qkv[j].astype(BF16)
            w_qv_t = jnp.concatenate([w[:, :D_MODEL], w[:, 2 * D_MODEL:]], axis=1).T
            q_t, k, v_t = _qkv_call(_qkv_a_kernel, "norm_qkv_rope_a", i, x, mod, norm1_g[i],
                                    w_qv_t, w[:, D_MODEL:2 * D_MODEL], [], tables_a, tables_a_t)
            o = _attn_a_call(i, q_t, k, v_t,
                             (a_lam_q1[j], a_lam_k1[j], a_lam_q2[j], a_lam_k2[j]), a_subln_g[j])
            w_o = a_w_o
        else:
            kv_w = B_N_KV * B_HEAD_DIM
            w = b_w_qkv[j].astype(BF16)
            w_qv_t = jnp.concatenate([w[:, :D_MODEL], w[:, D_MODEL + kv_w:]], axis=1).T
            q_gain = jnp.broadcast_to(b_qnorm_g[j][:, None], (B_HEAD_DIM, QKV_ROW_TILE))
            q_t, k, v_t = _qkv_call(_qkv_b_kernel, "norm_qkv_rope_b", i, x, mod, norm1_g[i],
                                    w_qv_t, w[:, D_MODEL:D_MODEL + kv_w],
                                    [q_gain, b_knorm_g[j].reshape(1, LANES)],
                                    tables_b, tables_b_t)
            o = _attn_b_call(q_t, k, v_t)
            w_o = b_w_o
        x = _post_call(i, i == DEPTH - 1, x, o, mod, w_o, j, norm2_g[i], f_w_in, f_w_out,
                       final_g)
    return x
```

```python
import functools
import math

import jax
import jax.numpy as jnp
import numpy as np
from jax.experimental import pallas as pl
from jax.experimental.pallas import tpu as pltpu

D_MODEL = 1024
DEPTH = 2
GRID_W = 64
EPS = 1e-6
LOG2E = math.log2(math.e)

A_HEAD_DIM = 64
A_N_HEADS = D_MODEL // (2 * A_HEAD_DIM)
A_ROT = A_HEAD_DIM // 4
ROPE_THETA_1D = 500000.0

B_HEAD_DIM = 128
B_N_HEADS = D_MODEL // B_HEAD_DIM
B_N_KV = max(1, B_N_HEADS // 4)
B_GROUP = B_N_HEADS // B_N_KV
ROPE_THETA_AXIAL = 10000.0

D_FF = -(-8 * D_MODEL // (3 * 256)) * 256

LANES = 128
SUBLANES = 8
VMEM_LIMIT_BYTES = 56 * 1024 * 1024

ROW_TILE = 512
QKV_ROW_TILE = 512
PROJ_WIDTH = 1024
Q_LANES = 1024
Q_TILE_A = Q_LANES // 2
Q_TILE_B = Q_LANES // B_GROUP
KV_CHUNK = 256
FF_CHUNK = 256
ACC_ROWS = LANES + SUBLANES
PIPELINE_LAG = 2

BF16 = jnp.bfloat16
F32 = jnp.float32


def _lambda_init(layer_idx):
    return 0.8 - 0.6 * math.exp(-0.3 * layer_idx)


def _rms_scale(x):
    return jax.lax.rsqrt(jnp.mean(x * x, axis=-1, keepdims=True) + EPS)


def _mod_kernel(c_ref, w_ref, b_ref, o_ref):
    c = c_ref[...]
    cond = c * jax.nn.sigmoid(c)
    y = jnp.dot(cond.astype(BF16), w_ref[0].astype(BF16), preferred_element_type=F32)
    o_ref[0] = y + b_ref[0]


def _mod_call(c, ada_w, ada_b):
    rows = c.shape[0]
    n_chunks = 6
    return pl.pallas_call(
        _mod_kernel,
        grid=(DEPTH, n_chunks),
        in_specs=[
            pl.BlockSpec((rows, D_MODEL), lambda i, j: (0, 0)),
            pl.BlockSpec((1, D_MODEL, D_MODEL), lambda i, j: (i, 0, j)),
            pl.BlockSpec((1, 1, D_MODEL), lambda i, j: (i, 0, j)),
        ],
        out_specs=pl.BlockSpec((1, rows, D_MODEL), lambda i, j: (i, 0, j)),
        out_shape=jax.ShapeDtypeStruct((DEPTH, rows, 6 * D_MODEL), F32),
        compiler_params=pltpu.CompilerParams(
            dimension_semantics=("arbitrary", "arbitrary"),
            vmem_limit_bytes=VMEM_LIMIT_BYTES),
        name="adaln_mod",
    )(c, ada_w, ada_b.reshape(DEPTH, 1, 6 * D_MODEL))


def _rope(x, cos_t, sin_fwd, sin_bwd, shift):
    fwd = pltpu.roll(x, LANES - shift, axis=1)
    bwd = pltpu.roll(x, shift, axis=1)
    return x * cos_t + fwd * sin_fwd + bwd * sin_bwd


def _modulated_norm(x, g, scale, shift):
    return (x * _rms_scale(x)) * (g * (1.0 + scale)) + shift


def _transposed_projection(w_t_ref, h, block):
    n_rows = w_t_ref.shape[0]
    h_t = h.T
    for r in range(0, n_rows, block):
        yield r, jnp.dot(w_t_ref[r:min(r + block, n_rows), :], h_t, preferred_element_type=F32)


def _qkv_a_kernel(x_ref, g_ref, shift_ref, scale_ref, wqv_t_ref, wk_ref, cos_ref, sf_ref, sb_ref,
                  cos_t_ref, sin_t_ref, qt_ref, k_ref, vt_ref):
    h = _modulated_norm(x_ref[0], g_ref[...], scale_ref[0], shift_ref[0]).astype(BF16)

    half = A_ROT // 2
    cos_t, sf, sb = cos_ref[...], sf_ref[...], sb_ref[...]
    for c in range(0, D_MODEL, PROJ_WIDTH):
        y = jnp.dot(h, wk_ref[:, c:c + PROJ_WIDTH], preferred_element_type=F32)
        for u in range(0, PROJ_WIDTH, LANES):
            k = _rope(y[:, u:u + LANES], cos_t, sf, sb, half)
            k_ref[0, :, c + u:c + u + LANES] = k.astype(BF16)

    cos_rows, sin_rows = cos_t_ref[...], sin_t_ref[...]
    for r, y in _transposed_projection(wqv_t_ref, h, PROJ_WIDTH):
        if r < D_MODEL:
            pieces = []
            for base in range(0, PROJ_WIDTH, A_HEAD_DIM):
                lo, hi = y[base:base + half], y[base + half:base + A_ROT]
                pieces += [lo * cos_rows - hi * sin_rows, hi * cos_rows + lo * sin_rows,
                           y[base + A_ROT:base + A_HEAD_DIM]]
            q_t = jnp.concatenate(pieces, axis=0) * (A_HEAD_DIM ** -0.5 * LOG2E)
            qt_ref[0, r:r + PROJ_WIDTH, :] = q_t.astype(BF16)
        else:
            vt_ref[0, r - D_MODEL:r - D_MODEL + PROJ_WIDTH, :] = y.astype(BF16)


def _qkv_b_kernel(x_ref, g_ref, shift_ref, scale_ref, wqv_t_ref, wk_ref, qg_ref, kg_ref,
                  cos_ref, sf_ref, sb_ref, cos_t_ref, sin_t_ref, qt_ref, k_ref, vt_ref):
    h = _modulated_norm(x_ref[0], g_ref[...], scale_ref[0], shift_ref[0]).astype(BF16)

    quarter = B_HEAD_DIM // 4
    cos_t, sf, sb = cos_ref[...], sf_ref[...], sb_ref[...]
    y = jnp.dot(h, wk_ref[...], preferred_element_type=F32)
    for u in range(0, wk_ref.shape[1], LANES):
        k = y[:, u:u + LANES]
        k = (k * _rms_scale(k)) * kg_ref[...]
        k = _rope(k, cos_t, sf, sb, quarter)
        k_ref[0, :, u:u + LANES] = k.astype(BF16)

    def swap_pairs(a):
        return jnp.concatenate(
            [a[quarter:2 * quarter], a[:quarter], a[3 * quarter:], a[2 * quarter:3 * quarter]],
            axis=0)

    q_gain = qg_ref[...]
    gain_cos = q_gain * cos_t_ref[...]
    gain_sin = swap_pairs(q_gain) * sin_t_ref[...]
    for r, y in _transposed_projection(wqv_t_ref, h, PROJ_WIDTH):
        if r < D_MODEL:
            for base in range(0, PROJ_WIDTH, B_HEAD_DIM):
                q = y[base:base + B_HEAD_DIM]
                q = q * jax.lax.rsqrt(jnp.mean(q * q, axis=0, keepdims=True) + EPS)
                q = q * gain_cos + swap_pairs(q) * gain_sin
                qt_ref[0, r + base:r + base + B_HEAD_DIM, :] = q.astype(BF16)
        else:
            vt_ref[0, r - D_MODEL:r - D_MODEL + y.shape[0], :] = y.astype(BF16)


def _mod_spec(mod, layer, chunk):
    batch = mod.shape[0] // DEPTH
    return pl.BlockSpec((1, 1, D_MODEL), lambda b, i: (layer * batch + b, 0, chunk))


def _const_spec(shape):
    return pl.BlockSpec(shape, lambda *_: (0,) * len(shape), pipeline_mode=pl.Buffered(1))


def _qkv_call(kernel_fn, name, layer, x, mod, norm_g, w_qv_t, w_k, extra, tables, tables_t):
    batch, seq, _ = x.shape
    n_q = D_MODEL
    n_v = w_qv_t.shape[0] - n_q
    n_k = w_k.shape[1]
    table_spec = pl.BlockSpec((QKV_ROW_TILE, LANES), lambda b, i: (i, 0))
    table_t_spec = pl.BlockSpec((tables_t[0].shape[0], QKV_ROW_TILE), lambda b, i: (0, i))
    in_specs = [
        pl.BlockSpec((1, QKV_ROW_TILE, D_MODEL), lambda b, i: (b, i, 0)),
        _const_spec((1, D_MODEL)),
        _mod_spec(mod, layer, 0),
        _mod_spec(mod, layer, 1),
        _const_spec(w_qv_t.shape),
        _const_spec(w_k.shape),
    ] + [_const_spec(e.shape) for e in extra] + [table_spec] * 3 + [table_t_spec] * 2
    return pl.pallas_call(
        kernel_fn,
        grid=(batch, seq // QKV_ROW_TILE),
        in_specs=in_specs,
        out_specs=[
            pl.BlockSpec((1, n_q, QKV_ROW_TILE), lambda b, i: (b, 0, i)),
            pl.BlockSpec((1, QKV_ROW_TILE, n_k), lambda b, i: (b, i, 0)),
            pl.BlockSpec((1, n_v, QKV_ROW_TILE), lambda b, i: (b, 0, i)),
        ],
        out_shape=[
            jax.ShapeDtypeStruct((batch, n_q, seq), BF16),
            jax.ShapeDtypeStruct((batch, seq, n_k), BF16),
            jax.ShapeDtypeStruct((batch, n_v, seq), BF16),
        ],
        compiler_params=pltpu.CompilerParams(
            dimension_semantics=("arbitrary", "arbitrary"),
            vmem_limit_bytes=VMEM_LIMIT_BYTES),
        name=name,
    )(x, norm_g.reshape(1, D_MODEL), mod, mod, w_qv_t, w_k, *extra, *tables, *tables_t)


def _scores_and_weighted_sum(q_t, k_ref, vt_ref, write, read, seq):
    s_w_ref, m_w_ref = write
    s_r_ref, m_r_ref = read
    n = s_w_ref.shape[1]
    groups = KV_CHUNK // SUBLANES
    m_prev = m_r_ref[0:1, :] if vt_ref is not None else None
    col_max = None
    col_sum = None
    acc = None
    for j in range(seq // KV_CHUNK):
        rows = slice(j * KV_CHUNK, (j + 1) * KV_CHUNK)
        if vt_ref is not None:
            p = jnp.exp2(s_r_ref[rows, :] - m_prev)
            part = jnp.sum(p.reshape(groups, SUBLANES, n), axis=0)
            col_sum = part if col_sum is None else col_sum + part
            pv = jnp.dot(vt_ref[0, :, rows], p.astype(BF16), preferred_element_type=F32)
            acc = pv if acc is None else acc + pv

        if q_t is not None:
            s = jnp.dot(k_ref[0, rows, :], q_t, preferred_element_type=F32)
            s_w_ref[rows, :] = s
            part = jnp.max(s.reshape(groups, SUBLANES, n), axis=0)
            col_max = part if col_max is None else jnp.maximum(col_max, part)
    if q_t is not None:
        m = jnp.max(col_max, axis=0, keepdims=True)
        m_w_ref[...] = jnp.broadcast_to(m, (SUBLANES, n))
    if vt_ref is not None:
        return jnp.concatenate([acc, col_sum], axis=0)
    return None


def _pipelined_step(n_tiles, buffers, step_fn):
    assert n_tiles % 2 == 0 and n_tiles > PIPELINE_LAG
    t = pl.program_id(0)
    even, odd = (buffers[0], buffers[1]), (buffers[1], buffers[0])
    full = jnp.logical_and(t >= PIPELINE_LAG, t < n_tiles)
    variants = [
        (t == 0, even, (True, False, False)),
        (t == 1, odd, (True, True, False)),
        (jnp.logical_and(full, t % 2 == 0), even, (True, True, True)),
        (jnp.logical_and(full, t % 2 == 1), odd, (True, True, True)),
        (t == n_tiles, even, (False, True, True)),
        (t == n_tiles + 1, odd, (False, False, True)),
    ]
    for condition, (write, read), stages in variants:
        pl.when(condition)(functools.partial(step_fn, write, read, *stages))


def _normalised(acc, cols):
    return acc[:LANES, cols] / jnp.sum(acc[LANES:, cols], axis=0, keepdims=True)


def _attn_a_kernel(lam_init, n_tiles, q_ref, k_ref, vt_ref, lq1_ref, lk1_ref, lq2_ref, lk2_ref,
                   g_ref, o_ref, *scratch):
    seq = k_ref.shape[1]
    tq = q_ref.shape[2]

    def step(write, read, scores, weighted, finish):
        if finish:
            acc_done = read[2][...]
            lam = (jnp.exp(jnp.sum(lq1_ref[...] * lk1_ref[...], axis=1, keepdims=True))
                   - jnp.exp(jnp.sum(lq2_ref[...] * lk2_ref[...], axis=1, keepdims=True))
                   + lam_init)
            o_t = (_normalised(acc_done, slice(0, tq))
                   - lam * _normalised(acc_done, slice(tq, 2 * tq)))
            o = o_t.T
            o = (o * _rms_scale(o)) * g_ref[...]
            o_ref[0] = (o * (1.0 - lam_init)).astype(BF16)

        q_t = None
        if scores:
            q = q_ref[0]
            row = jax.lax.broadcasted_iota(jnp.int32, q.shape, 0)
            zero = jnp.zeros_like(q)
            q_t = jnp.concatenate([jnp.where(row < A_HEAD_DIM, q, zero),
                                   jnp.where(row >= A_HEAD_DIM, q, zero)], axis=1)
        if scores or weighted:
            acc = _scores_and_weighted_sum(q_t, k_ref, vt_ref if weighted else None,
                                           write[:2], read[:2], seq)
            if weighted:
                write[2][...] = acc

    _pipelined_step(n_tiles, (scratch[:3], scratch[3:]), step)


def _attn_b_kernel(n_tiles, q_ref, k_ref, vt_ref, o_ref, *scratch):
    seq = k_ref.shape[1]
    tq = q_ref.shape[2]

    def step(write, read, scores, weighted, finish):
        if finish:
            acc_done = read[2][...]
            for g in range(B_GROUP):
                o_t = _normalised(acc_done, slice(g * tq, (g + 1) * tq))
                o_ref[0, :, g * LANES:(g + 1) * LANES] = o_t.T.astype(BF16)

        q_t = None
        if scores:
            q = q_ref[0]
            q_t = jnp.concatenate([q[g * LANES:(g + 1) * LANES, :] for g in range(B_GROUP)],
                                  axis=1)
        if scores or weighted:
            acc = _scores_and_weighted_sum(q_t, k_ref, vt_ref if weighted else None,
                                           write[:2], read[:2], seq)
            if weighted:
                write[2][...] = acc

    _pipelined_step(n_tiles, (scratch[:3], scratch[3:]), step)


def _attn_scratch(seq):
    buffer_set = [pltpu.VMEM((seq, Q_LANES), F32),
                  pltpu.VMEM((SUBLANES, Q_LANES), F32),
                  pltpu.VMEM((ACC_ROWS, Q_LANES), F32)]
    return buffer_set + buffer_set


def _attn_call(kernel_fn, name, q_t, k, v_t, n_heads, q_rows, tile, extra):
    batch, seq, _ = k.shape
    tiles_per_head = seq // tile
    n_tiles = batch * n_heads * tiles_per_head

    def split(t):
        t = jnp.clip(t, 0, n_tiles - 1)
        return (t // (n_heads * tiles_per_head), (t // tiles_per_head) % n_heads,
                t % tiles_per_head)

    def q_map(t):
        b, h, i = split(t)
        return (b, h, i)

    def k_map(t):
        b, h, _ = split(t)
        return (b, 0, h)

    def v_map(t):
        b, h, _ = split(t - 1)
        return (b, h, 0)

    def o_map(t):
        b, h, i = split(t - PIPELINE_LAG)
        return (b, i, h)

    return pl.pallas_call(
        functools.partial(kernel_fn, n_tiles),
        grid=(n_tiles + PIPELINE_LAG,),
        in_specs=[pl.BlockSpec((1, q_rows, tile), q_map),
                  pl.BlockSpec((1, seq, LANES), k_map),
                  pl.BlockSpec((1, LANES, seq), v_map)] + [_const_spec(e.shape) for e in extra],
        out_specs=pl.BlockSpec((1, tile, D_MODEL // n_heads), o_map),
        out_shape=jax.ShapeDtypeStruct((batch, seq, D_MODEL), BF16),
        scratch_shapes=_attn_scratch(seq),
        compiler_params=pltpu.CompilerParams(
            dimension_semantics=("arbitrary",),
            vmem_limit_bytes=VMEM_LIMIT_BYTES),
        name=name,
    )(q_t, k, v_t, *extra)


def _attn_a_call(layer, q_t, k, v_t, lam_params, subln_g):
    extra = [p.reshape(1, A_HEAD_DIM) for p in lam_params] + [subln_g.reshape(1, LANES)]
    return _attn_call(functools.partial(_attn_a_kernel, _lambda_init(layer)), "diff_attention",
                      q_t, k, v_t, A_N_HEADS, LANES, Q_TILE_A, extra)


def _attn_b_call(q_t, k, v_t):
    return _attn_call(_attn_b_kernel, "axial_gqa_attention", q_t, k, v_t, B_N_KV,
                      B_GROUP * B_HEAD_DIM, Q_TILE_B, [])


def _post_kernel(final_norm, layer, w_o_layer, x_ref, o_ref, wo_hbm, gate1_ref, g2_ref,
                 shift2_ref, scale2_ref, gate2_ref, win_hbm, wout_hbm, fg_ref, out_ref,
                 act_ref, wo_ref, win_ref, wout_ref, sems):
    n_chunks = D_FF // FF_CHUNK

    def cols(c, up):
        start = (D_FF if up else 0) + c * FF_CHUNK
        return slice(start, start + FF_CHUNK)

    pieces = [(wo_hbm.at[w_o_layer], wo_ref)]
    for c in range(n_chunks):
        for up in (False, True):
            pieces.append((win_hbm.at[layer, :, cols(c, up)], win_ref.at[:, cols(c, up)]))
    pieces.append((wout_hbm.at[layer], wout_ref))
    copies = [pltpu.make_async_copy(src, dst, sems.at[i]) for i, (src, dst) in enumerate(pieces)]

    def step(arrived):
        arrived(0)
        y = jnp.dot(o_ref[0], wo_ref[...].astype(BF16), preferred_element_type=F32)
        x1 = x_ref[0] + gate1_ref[0] * y
        out_ref[0] = x1
        h = _modulated_norm(x1, g2_ref[...], scale2_ref[0], shift2_ref[0]).astype(BF16)
        for c in range(n_chunks):
            arrived(1 + 2 * c)
            arrived(2 + 2 * c)
            gate = jnp.dot(h, win_ref[:, cols(c, False)].astype(BF16), preferred_element_type=F32)
            up = jnp.dot(h, win_ref[:, cols(c, True)].astype(BF16), preferred_element_type=F32)
            act_ref[:, cols(c, False)] = ((gate * jax.nn.sigmoid(gate)) * up).astype(BF16)
        arrived(len(copies) - 1)
        f = jnp.dot(act_ref[...], wout_ref[...].astype(BF16), preferred_element_type=F32)
        x2 = out_ref[0] + gate2_ref[0] * f
        if final_norm:
            x2 = (x2 * _rms_scale(x2)) * fg_ref[...]
        out_ref[0] = x2

    first = jnp.logical_and(pl.program_id(0) == 0, pl.program_id(1) == 0)

    @pl.when(first)
    def _():
        for copy in copies:
            copy.start()
        step(lambda i: copies[i].wait())

    @pl.when(jnp.logical_not(first))
    def _():
        step(lambda i: None)


def _post_call(layer, final_norm, x, o, mod, w_o, w_o_layer, norm2_g, w_in, w_out, final_g):
    batch, seq, _ = x.shape
    row_spec = pl.BlockSpec((1, ROW_TILE, D_MODEL), lambda b, i: (b, i, 0))
    hbm_spec = pl.BlockSpec(memory_space=pl.ANY)
    return pl.pallas_call(
        functools.partial(_post_kernel, final_norm, layer, w_o_layer),
        grid=(batch, seq // ROW_TILE),
        in_specs=[
            row_spec,
            row_spec,
            hbm_spec,
            _mod_spec(mod, layer, 2),
            _const_spec((1, D_MODEL)),
            _mod_spec(mod, layer, 3),
            _mod_spec(mod, layer, 4),
            _mod_spec(mod, layer, 5),
            hbm_spec,
            hbm_spec,
            _const_spec((1, D_MODEL)),
        ],
        out_specs=row_spec,
        out_shape=jax.ShapeDtypeStruct((batch, seq, D_MODEL), F32),
        scratch_shapes=[pltpu.VMEM((ROW_TILE, D_FF), BF16),
                        pltpu.VMEM(w_o.shape[1:], w_o.dtype),
                        pltpu.VMEM(w_in.shape[1:], w_in.dtype),
                        pltpu.VMEM(w_out.shape[1:], w_out.dtype),
                        pltpu.SemaphoreType.DMA((2 * (D_FF // FF_CHUNK) + 2,))],
        compiler_params=pltpu.CompilerParams(
            dimension_semantics=("arbitrary", "arbitrary"),
            vmem_limit_bytes=VMEM_LIMIT_BYTES),
        name="wo_ffn",
    )(x, o, w_o, mod, norm2_g.reshape(1, D_MODEL), mod, mod, mod, w_in, w_out,
      final_g.reshape(1, D_MODEL))


def _rope_angles(pos, half, theta):
    freqs = float(theta) ** (-np.arange(half, dtype=np.float64) / half)
    ang = np.asarray(pos, np.float64)[:, None] * freqs[None, :]
    return np.cos(ang), np.sin(ang)


def _rope_tables(pos_per_lane_group, half, theta, period, seq):
    cos_cols, sf_cols, sb_cols = [], [], []
    for pos in pos_per_lane_group:
        c, s = _rope_angles(pos, half, theta)
        z = np.zeros_like(s)
        cos_cols += [c, c]
        sf_cols += [-s, z]
        sb_cols += [z, s]
    rest = period - 2 * half * len(pos_per_lane_group)
    if rest:
        cos_cols.append(np.ones((seq, rest)))
        sf_cols.append(np.zeros((seq, rest)))
        sb_cols.append(np.zeros((seq, rest)))
    reps = LANES // period
    return tuple(_table(np.tile(np.concatenate(cols, axis=1), (1, reps)))
                 for cols in (cos_cols, sf_cols, sb_cols))


def _table(values):
    return jnp.asarray(np.ascontiguousarray(values, dtype=np.float32))


def kernel(x, c, ada_w, ada_b, norm1_g, norm2_g, a_w_qkv, a_w_o, a_lam_q1, a_lam_k1, a_lam_q2,
           a_lam_k2, a_subln_g, b_w_qkv, b_w_o, b_qnorm_g, b_knorm_g, f_w_in, f_w_out, final_g):
    batch, seq, _ = x.shape
    assert seq % QKV_ROW_TILE == 0 and seq % ROW_TILE == 0
    assert seq % KV_CHUNK == 0 and seq % Q_LANES == 0

    t = np.arange(seq)
    row_pos, col_pos = t // GRID_W, t % GRID_W
    tables_a = _rope_tables([t], A_ROT // 2, ROPE_THETA_1D, A_HEAD_DIM, seq)
    tables_b = _rope_tables([row_pos, col_pos], B_HEAD_DIM // 4, ROPE_THETA_AXIAL,
                            B_HEAD_DIM, seq)
    cos_a, sin_a = _rope_angles(t, A_ROT // 2, ROPE_THETA_1D)
    tables_a_t = (_table(cos_a.T), _table(sin_a.T))
    cos_r, sin_r = _rope_angles(row_pos, B_HEAD_DIM // 4, ROPE_THETA_AXIAL)
    cos_c, sin_c = _rope_angles(col_pos, B_HEAD_DIM // 4, ROPE_THETA_AXIAL)
    q_scale_b = B_HEAD_DIM ** -0.5 * LOG2E
    tables_b_t = (_table(q_scale_b * np.concatenate([cos_r, cos_r, cos_c, cos_c], axis=1).T),
                  _table(q_scale_b * np.concatenate([-sin_r, sin_r, -sin_c, sin_c], axis=1).T))

    mod = _mod_call(c, ada_w, ada_b).reshape(DEPTH * batch, 1, 6 * D_MODEL)

    for i in range(DEPTH):
        j = i // 2
        if i % 2 == 0:
            w = a_w_qkv[j].astype(BF16)
            w_qv_t = jnp.concatenate([w[:, :D_MODEL], w[:, 2 * D_MODEL:]], axis=1).T
            q_t, k, v_t = _qkv_call(_qkv_a_kernel, "norm_qkv_rope_a", i, x, mod, norm1_g[i],
                                    w_qv_t, w[:, D_MODEL:2 * D_MODEL], [], tables_a, tables_a_t)
            o = _attn_a_call(i, q_t, k, v_t,
                             (a_lam_q1[j], a_lam_k1[j], a_lam_q2[j], a_lam_k2[j]), a_subln_g[j])
            w_o = a_w_o
        else:
            kv_w = B_N_KV * B_HEAD_DIM
            w = b_w_qkv[j].astype(BF16)
            w_qv_t = jnp.concatenate([w[:, :D_MODEL], w[:, D_MODEL + kv_w:]], axis=1).T
            q_gain = jnp.broadcast_to(b_qnorm_g[j][:, None], (B_HEAD_DIM, QKV_ROW_TILE))
            q_t, k, v_t = _qkv_call(_qkv_b_kernel, "norm_qkv_rope_b", i, x, mod, norm1_g[i],
                                    w_qv_t, w[:, D_MODEL:D_MODEL + kv_w],
                                    [q_gain, b_knorm_g[j].reshape(1, LANES)],
                                    tables_b, tables_b_t)
            o = _attn_b_call(q_t, k, v_t)
            w_o = b_w_o
        x = _post_call(i, i == DEPTH - 1, x, o, mod, w_o, j, norm2_g[i], f_w_in, f_w_out,
                       final_g)
    return x
```

```python
import functools
import math

import jax
import jax.numpy as jnp
import numpy as np
from jax.experimental import pallas as pl
from jax.experimental.pallas import tpu as pltpu

D_MODEL = 1024
DEPTH = 2
GRID_W = 64
EPS = 1e-6
LOG2E = math.log2(math.e)

A_HEAD_DIM = 64
A_N_HEADS = D_MODEL // (2 * A_HEAD_DIM)
A_ROT = A_HEAD_DIM // 4
ROPE_THETA_1D = 500000.0

B_HEAD_DIM = 128
B_N_HEADS = D_MODEL // B_HEAD_DIM
B_N_KV = max(1, B_N_HEADS // 4)
B_GROUP = B_N_HEADS // B_N_KV
ROPE_THETA_AXIAL = 10000.0

D_FF = -(-8 * D_MODEL // (3 * 256)) * 256

LANES = 128
SUBLANES = 8
VMEM_LIMIT_BYTES = 56 * 1024 * 1024

ROW_TILE = 512
QKV_ROW_TILE = 512
PROJ_WIDTH = 1024
Q_LANES = 1024
Q_TILE_A = Q_LANES // 2
Q_TILE_B = Q_LANES // B_GROUP
KV_CHUNK = 256
FF_CHUNK = 256
ONES_ROWS = 16
V_ROWS = LANES + ONES_ROWS
PIPELINE_LAG = 2

BF16 = jnp.bfloat16
F32 = jnp.float32


def _lambda_init(layer_idx):
    return 0.8 - 0.6 * math.exp(-0.3 * layer_idx)


def _rms_scale(x):
    return jax.lax.rsqrt(jnp.mean(x * x, axis=-1, keepdims=True) + EPS)


def _mod_kernel(c_ref, w_ref, b_ref, o_ref):
    c = c_ref[...]
    cond = c * jax.nn.sigmoid(c)
    y = jnp.dot(cond.astype(BF16), w_ref[0].astype(BF16), preferred_element_type=F32)
    o_ref[0] = y + b_ref[0]


def _mod_call(c, ada_w, ada_b):
    rows = c.shape[0]
    n_chunks = 6
    return pl.pallas_call(
        _mod_kernel,
        grid=(DEPTH, n_chunks),
        in_specs=[
            pl.BlockSpec((rows, D_MODEL), lambda i, j: (0, 0)),
            pl.BlockSpec((1, D_MODEL, D_MODEL), lambda i, j: (i, 0, j)),
            pl.BlockSpec((1, 1, D_MODEL), lambda i, j: (i, 0, j)),
        ],
        out_specs=pl.BlockSpec((1, rows, D_MODEL), lambda i, j: (i, 0, j)),
        out_shape=jax.ShapeDtypeStruct((DEPTH, rows, 6 * D_MODEL), F32),
        compiler_params=pltpu.CompilerParams(
            dimension_semantics=("arbitrary", "arbitrary"),
            vmem_limit_bytes=VMEM_LIMIT_BYTES),
        name="adaln_mod",
    )(c, ada_w, ada_b.reshape(DEPTH, 1, 6 * D_MODEL))


def _rope(x, cos_t, sin_fwd, sin_bwd, shift):
    fwd = pltpu.roll(x, LANES - shift, axis=1)
    bwd = pltpu.roll(x, shift, axis=1)
    return x * cos_t + fwd * sin_fwd + bwd * sin_bwd


def _modulated_norm(x, g, scale, shift):
    return (x * _rms_scale(x)) * (g * (1.0 + scale)) + shift


def _transposed_projection(w_t_ref, h, block):
    n_rows = w_t_ref.shape[0]
    h_t = h.T
    for r in range(0, n_rows, block):
        yield r, jnp.dot(w_t_ref[r:min(r + block, n_rows), :], h_t, preferred_element_type=F32)


def _store_v_heads(vt_ref, first_head, y):
    ones = jnp.ones((ONES_ROWS, y.shape[1]), BF16)
    for u in range(y.shape[0] // LANES):
        base = (first_head + u) * V_ROWS
        vt_ref[0, base:base + LANES, :] = y[u * LANES:(u + 1) * LANES].astype(BF16)
        vt_ref[0, base + LANES:base + V_ROWS, :] = ones


def _qkv_a_kernel(x_ref, g_ref, shift_ref, scale_ref, wqv_t_ref, wk_ref, cos_ref, sf_ref, sb_ref,
                  cos_t_ref, sin_t_ref, qt_ref, k_ref, vt_ref):
    h = _modulated_norm(x_ref[0], g_ref[...], scale_ref[0], shift_ref[0]).astype(BF16)

    half = A_ROT // 2
    cos_t, sf, sb = cos_ref[...], sf_ref[...], sb_ref[...]
    for c in range(0, D_MODEL, PROJ_WIDTH):
        y = jnp.dot(h, wk_ref[:, c:c + PROJ_WIDTH], preferred_element_type=F32)
        for u in range(0, PROJ_WIDTH, LANES):
            k = _rope(y[:, u:u + LANES], cos_t, sf, sb, half)
            k_ref[0, :, c + u:c + u + LANES] = k.astype(BF16)

    cos_rows, sin_rows = cos_t_ref[...], sin_t_ref[...]
    for r, y in _transposed_projection(wqv_t_ref, h, PROJ_WIDTH):
        if r < D_MODEL:
            pieces = []
            for base in range(0, PROJ_WIDTH, A_HEAD_DIM):
                lo, hi = y[base:base + half], y[base + half:base + A_ROT]
                pieces += [lo * cos_rows - hi * sin_rows, hi * cos_rows + lo * sin_rows,
                           y[base + A_ROT:base + A_HEAD_DIM]]
            q_t = jnp.concatenate(pieces, axis=0) * (A_HEAD_DIM ** -0.5 * LOG2E)
            qt_ref[0, r:r + PROJ_WIDTH, :] = q_t.astype(BF16)
        else:
            _store_v_heads(vt_ref, (r - D_MODEL) // LANES, y)


def _qkv_b_kernel(x_ref, g_ref, shift_ref, scale_ref, wqv_t_ref, wk_ref, qg_ref, kg_ref,
                  cos_ref, sf_ref, sb_ref, cos_t_ref, sin_t_ref, qt_ref, k_ref, vt_ref):
    h = _modulated_norm(x_ref[0], g_ref[...], scale_ref[0], shift_ref[0]).astype(BF16)

    quarter = B_HEAD_DIM // 4
    cos_t, sf, sb = cos_ref[...], sf_ref[...], sb_ref[...]
    y = jnp.dot(h, wk_ref[...], preferred_element_type=F32)
    for u in range(0, wk_ref.shape[1], LANES):
        k = y[:, u:u + LANES]
        k = (k * _rms_scale(k)) * kg_ref[...]
        k = _rope(k, cos_t, sf, sb, quarter)
        k_ref[0, :, u:u + LANES] = k.astype(BF16)

    def swap_pairs(a):
        return jnp.concatenate(
            [a[quarter:2 * quarter], a[:quarter], a[3 * quarter:], a[2 * quarter:3 * quarter]],
            axis=0)

    q_gain = qg_ref[...]
    gain_cos = q_gain * cos_t_ref[...]
    gain_sin = swap_pairs(q_gain) * sin_t_ref[...]
    for r, y in _transposed_projection(wqv_t_ref, h, PROJ_WIDTH):
        if r < D_MODEL:
            for base in range(0, PROJ_WIDTH, B_HEAD_DIM):
                q = y[base:base + B_HEAD_DIM]
                q = q * jax.lax.rsqrt(jnp.mean(q * q, axis=0, keepdims=True) + EPS)
                q = q * gain_cos + swap_pairs(q) * gain_sin
                qt_ref[0, r + base:r + base + B_HEAD_DIM, :] = q.astype(BF16)
        else:
            _store_v_heads(vt_ref, (r - D_MODEL) // LANES, y)


def _mod_spec(mod, layer, chunk):
    batch = mod.shape[0] // DEPTH
    return pl.BlockSpec((1, 1, D_MODEL), lambda b, i: (layer * batch + b, 0, chunk))


def _const_spec(shape):
    return pl.BlockSpec(shape, lambda *_: (0,) * len(shape), pipeline_mode=pl.Buffered(1))


def _qkv_call(kernel_fn, name, layer, x, mod, norm_g, w_qv_t, w_k, extra, tables, tables_t):
    batch, seq, _ = x.shape
    n_q = D_MODEL
    n_v = (w_qv_t.shape[0] - n_q) // LANES * V_ROWS
    n_k = w_k.shape[1]
    table_spec = pl.BlockSpec((QKV_ROW_TILE, LANES), lambda b, i: (i, 0))
    table_t_spec = pl.BlockSpec((tables_t[0].shape[0], QKV_ROW_TILE), lambda b, i: (0, i))
    in_specs = [
        pl.BlockSpec((1, QKV_ROW_TILE, D_MODEL), lambda b, i: (b, i, 0)),
        _const_spec((1, D_MODEL)),
        _mod_spec(mod, layer, 0),
        _mod_spec(mod, layer, 1),
        _const_spec(w_qv_t.shape),
        _const_spec(w_k.shape),
    ] + [_const_spec(e.shape) for e in extra] + [table_spec] * 3 + [table_t_spec] * 2
    return pl.pallas_call(
        kernel_fn,
        grid=(batch, seq // QKV_ROW_TILE),
        in_specs=in_specs,
        out_specs=[
            pl.BlockSpec((1, n_q, QKV_ROW_TILE), lambda b, i: (b, 0, i)),
            pl.BlockSpec((1, QKV_ROW_TILE, n_k), lambda b, i: (b, i, 0)),
            pl.BlockSpec((1, n_v, QKV_ROW_TILE), lambda b, i: (b, 0, i)),
        ],
        out_shape=[
            jax.ShapeDtypeStruct((batch, n_q, seq), BF16),
            jax.ShapeDtypeStruct((batch, seq, n_k), BF16),
            jax.ShapeDtypeStruct((batch, n_v, seq), BF16),
        ],
        compiler_params=pltpu.CompilerParams(
            dimension_semantics=("arbitrary", "arbitrary"),
            vmem_limit_bytes=VMEM_LIMIT_BYTES),
        name=name,
    )(x, norm_g.reshape(1, D_MODEL), mod, mod, w_qv_t, w_k, *extra, *tables, *tables_t)


def _scores_and_weighted_sum(q_t, k_ref, vt_ref, write, read, seq):
    s_w_ref, m_w_ref = write
    s_r_ref, m_r_ref = read
    n = s_w_ref.shape[1]
    groups = KV_CHUNK // SUBLANES
    m_prev = m_r_ref[0:1, :] if vt_ref is not None else None
    col_max = None
    acc = None
    for j in range(seq // KV_CHUNK):
        rows = slice(j * KV_CHUNK, (j + 1) * KV_CHUNK)
        if vt_ref is not None:
            p = jnp.exp2(s_r_ref[rows, :] - m_prev).astype(BF16)
            pv = jnp.dot(vt_ref[0, :, rows], p, preferred_element_type=F32)
            acc = pv if acc is None else acc + pv

        if q_t is not None:
            s = jnp.dot(k_ref[0, rows, :], q_t, preferred_element_type=F32)
            s_w_ref[rows, :] = s
            part = jnp.max(s.reshape(groups, SUBLANES, n), axis=0)
            col_max = part if col_max is None else jnp.maximum(col_max, part)
    if q_t is not None:
        m = jnp.max(col_max, axis=0, keepdims=True)
        m_w_ref[...] = jnp.broadcast_to(m, (SUBLANES, n))
    return acc


def _pipelined_step(n_tiles, buffers, step_fn):
    assert n_tiles % 2 == 0 and n_tiles > PIPELINE_LAG
    t = pl.program_id(0)
    even, odd = (buffers[0], buffers[1]), (buffers[1], buffers[0])
    full = jnp.logical_and(t >= PIPELINE_LAG, t < n_tiles)
    variants = [
        (t == 0, even, (True, False, False)),
        (t == 1, odd, (True, True, False)),
        (jnp.logical_and(full, t % 2 == 0), even, (True, True, True)),
        (jnp.logical_and(full, t % 2 == 1), odd, (True, True, True)),
        (t == n_tiles, even, (False, True, True)),
        (t == n_tiles + 1, odd, (False, False, True)),
    ]
    for condition, (write, read), stages in variants:
        pl.when(condition)(functools.partial(step_fn, write, read, *stages))


def _normalised(acc, cols):
    return acc[:LANES, cols] / acc[LANES:LANES + 1, cols]


def _attn_a_kernel(lam_init, n_tiles, q_ref, k_ref, vt_ref, lq1_ref, lk1_ref, lq2_ref, lk2_ref,
                   g_ref, o_ref, *scratch):
    seq = k_ref.shape[1]
    tq = q_ref.shape[2]

    def step(write, read, scores, weighted, finish):
        if finish:
            acc_done = read[2][...]
            lam = (jnp.exp(jnp.sum(lq1_ref[...] * lk1_ref[...], axis=1, keepdims=True))
                   - jnp.exp(jnp.sum(lq2_ref[...] * lk2_ref[...], axis=1, keepdims=True))
                   + lam_init)
            o_t = (_normalised(acc_done, slice(0, tq))
                   - lam * _normalised(acc_done, slice(tq, 2 * tq)))
            o = o_t.T
            o = (o * _rms_scale(o)) * g_ref[...]
            o_ref[0] = (o * (1.0 - lam_init)).astype(BF16)

        q_t = None
        if scores:
            q = q_ref[0]
            row = jax.lax.broadcasted_iota(jnp.int32, q.shape, 0)
            zero = jnp.zeros_like(q)
            q_t = jnp.concatenate([jnp.where(row < A_HEAD_DIM, q, zero),
                                   jnp.where(row >= A_HEAD_DIM, q, zero)], axis=1)
        if scores or weighted:
            acc = _scores_and_weighted_sum(q_t, k_ref, vt_ref if weighted else None,
                                           write[:2], read[:2], seq)
            if weighted:
                write[2][...] = acc

    _pipelined_step(n_tiles, (scratch[:3], scratch[3:]), step)


def _attn_b_kernel(n_tiles, q_ref, k_ref, vt_ref, o_ref, *scratch):
    seq = k_ref.shape[1]
    tq = q_ref.shape[2]

    def step(write, read, scores, weighted, finish):
        if finish:
            acc_done = read[2][...]
            for g in range(B_GROUP):
                o_t = _normalised(acc_done, slice(g * tq, (g + 1) * tq))
                o_ref[0, :, g * LANES:(g + 1) * LANES] = o_t.T.astype(BF16)

        q_t = None
        if scores:
            q = q_ref[0]
            q_t = jnp.concatenate([q[g * LANES:(g + 1) * LANES, :] for g in range(B_GROUP)],
                                  axis=1)
        if scores or weighted:
            acc = _scores_and_weighted_sum(q_t, k_ref, vt_ref if weighted else None,
                                           write[:2], read[:2], seq)
            if weighted:
                write[2][...] = acc

    _pipelined_step(n_tiles, (scratch[:3], scratch[3:]), step)


def _attn_scratch(seq):
    buffer_set = [pltpu.VMEM((seq, Q_LANES), F32),
                  pltpu.VMEM((SUBLANES, Q_LANES), F32),
                  pltpu.VMEM((V_ROWS, Q_LANES), F32)]
    return buffer_set + buffer_set


def _attn_call(kernel_fn, name, q_t, k, v_t, n_heads, q_rows, tile, extra):
    batch, seq, _ = k.shape
    tiles_per_head = seq // tile
    n_tiles = batch * n_heads * tiles_per_head

    def split(t):
        t = jnp.clip(t, 0, n_tiles - 1)
        return (t // (n_heads * tiles_per_head), (t // tiles_per_head) % n_heads,
                t % tiles_per_head)

    def q_map(t):
        b, h, i = split(t)
        return (b, h, i)

    def k_map(t):
        b, h, _ = split(t)
        return (b, 0, h)

    def v_map(t):
        b, h, _ = split(t - 1)
        return (b, h, 0)

    def o_map(t):
        b, h, i = split(t - PIPELINE_LAG)
        return (b, i, h)

    return pl.pallas_call(
        functools.partial(kernel_fn, n_tiles),
        grid=(n_tiles + PIPELINE_LAG,),
        in_specs=[pl.BlockSpec((1, q_rows, tile), q_map),
                  pl.BlockSpec((1, seq, LANES), k_map),
                  pl.BlockSpec((1, V_ROWS, seq), v_map)] + [_const_spec(e.shape) for e in extra],
        out_specs=pl.BlockSpec((1, tile, D_MODEL // n_heads), o_map),
        out_shape=jax.ShapeDtypeStruct((batch, seq, D_MODEL), BF16),
        scratch_shapes=_attn_scratch(seq),
        compiler_params=pltpu.CompilerParams(
            dimension_semantics=("arbitrary",),
            vmem_limit_bytes=VMEM_LIMIT_BYTES),
        name=name,
    )(q_t, k, v_t, *extra)


def _attn_a_call(layer, q_t, k, v_t, lam_params, subln_g):
    extra = [p.reshape(1, A_HEAD_DIM) for p in lam_params] + [subln_g.reshape(1, LANES)]
    return _attn_call(functools.partial(_attn_a_kernel, _lambda_init(layer)), "diff_attention",
                      q_t, k, v_t, A_N_HEADS, LANES, Q_TILE_A, extra)


def _attn_b_call(q_t, k, v_t):
    return _attn_call(_attn_b_kernel, "axial_gqa_attention", q_t, k, v_t, B_N_KV,
                      B_GROUP * B_HEAD_DIM, Q_TILE_B, [])


def _post_kernel(final_norm, layer, w_o_layer, x_ref, o_ref, wo_hbm, gate1_ref, g2_ref,
                 shift2_ref, scale2_ref, gate2_ref, win_hbm, wout_hbm, fg_ref, out_ref,
                 act_ref, wo_ref, win_ref, wout_ref, sems):
    n_chunks = D_FF // FF_CHUNK

    def cols(c, up):
        start = (D_FF if up else 0) + c * FF_CHUNK
        return slice(start, start + FF_CHUNK)

    pieces = [(wo_hbm.at[w_o_layer], wo_ref)]
    for c in range(n_chunks):
        for up in (False, True):
            pieces.append((win_hbm.at[layer, :, cols(c, up)], win_ref.at[:, cols(c, up)]))
    pieces.append((wout_hbm.at[layer], wout_ref))
    copies = [pltpu.make_async_copy(src, dst, sems.at[i]) for i, (src, dst) in enumerate(pieces)]

    def step(arrived):
        arrived(0)
        y = jnp.dot(o_ref[0], wo_ref[...].astype(BF16), preferred_element_type=F32)
        x1 = x_ref[0] + gate1_ref[0] * y
        out_ref[0] = x1
        h = _modulated_norm(x1, g2_ref[...], scale2_ref[0], shift2_ref[0]).astype(BF16)
        for c in range(n_chunks):
            arrived(1 + 2 * c)
            arrived(2 + 2 * c)
            gate = jnp.dot(h, win_ref[:, cols(c, False)].astype(BF16), preferred_element_type=F32)
            up = jnp.dot(h, win_ref[:, cols(c, True)].astype(BF16), preferred_element_type=F32)
            act_ref[:, cols(c, False)] = ((gate * jax.nn.sigmoid(gate)) * up).astype(BF16)
        arrived(len(copies) - 1)
        f = jnp.dot(act_ref[...], wout_ref[...].astype(BF16), preferred_element_type=F32)
        x2 = out_ref[0] + gate2_ref[0] * f
        if final_norm:
            x2 = (x2 * _rms_scale(x2)) * fg_ref[...]
        out_ref[0] = x2

    first = jnp.logical_and(pl.program_id(0) == 0, pl.program_id(1) == 0)

    @pl.when(first)
    def _():
        for copy in copies:
            copy.start()
        step(lambda i: copies[i].wait())

    @pl.when(jnp.logical_not(first))
    def _():
        step(lambda i: None)


def _post_call(layer, final_norm, x, o, mod, w_o, w_o_layer, norm2_g, w_in, w_out, final_g):
    batch, seq, _ = x.shape
    row_spec = pl.BlockSpec((1, ROW_TILE, D_MODEL), lambda b, i: (b, i, 0))
    hbm_spec = pl.BlockSpec(memory_space=pl.ANY)
    return pl.pallas_call(
        functools.partial(_post_kernel, final_norm, layer, w_o_layer),
        grid=(batch, seq // ROW_TILE),
        in_specs=[
            row_spec,
            row_spec,
            hbm_spec,
            _mod_spec(mod, layer, 2),
            _const_spec((1, D_MODEL)),
            _mod_spec(mod, layer, 3),
            _mod_spec(mod, layer, 4),
            _mod_spec(mod, layer, 5),
            hbm_spec,
            hbm_spec,
            _const_spec((1, D_MODEL)),
        ],
        out_specs=row_spec,
        out_shape=jax.ShapeDtypeStruct((batch, seq, D_MODEL), F32),
        scratch_shapes=[pltpu.VMEM((ROW_TILE, D_FF), BF16),
                        pltpu.VMEM(w_o.shape[1:], w_o.dtype),
                        pltpu.VMEM(w_in.shape[1:], w_in.dtype),
                        pltpu.VMEM(w_out.shape[1:], w_out.dtype),
                        pltpu.SemaphoreType.DMA((2 * (D_FF // FF_CHUNK) + 2,))],
        compiler_params=pltpu.CompilerParams(
            dimension_semantics=("arbitrary", "arbitrary"),
            vmem_limit_bytes=VMEM_LIMIT_BYTES),
        name="wo_ffn",
    )(x, o, w_o, mod, norm2_g.reshape(1, D_MODEL), mod, mod, mod, w_in, w_out,
      final_g.reshape(1, D_MODEL))


def _rope_angles(pos, half, theta):
    freqs = float(theta) ** (-np.arange(half, dtype=np.float64) / half)
    ang = np.asarray(pos, np.float64)[:, None] * freqs[None, :]
    return np.cos(ang), np.sin(ang)


def _rope_tables(pos_per_lane_group, half, theta, period, seq):
    cos_cols, sf_cols, sb_cols = [], [], []
    for pos in pos_per_lane_group:
        c, s = _rope_angles(pos, half, theta)
        z = np.zeros_like(s)
        cos_cols += [c, c]
        sf_cols += [-s, z]
        sb_cols += [z, s]
    rest = period - 2 * half * len(pos_per_lane_group)
    if rest:
        cos_cols.append(np.ones((seq, rest)))
        sf_cols.append(np.zeros((seq, rest)))
        sb_cols.append(np.zeros((seq, rest)))
    reps = LANES // period
    return tuple(_table(np.tile(np.concatenate(cols, axis=1), (1, reps)))
                 for cols in (cos_cols, sf_cols, sb_cols))


def _table(values):
    return jnp.asarray(np.ascontiguousarray(values, dtype=np.float32))


def kernel(x, c, ada_w, ada_b, norm1_g, norm2_g, a_w_qkv, a_w_o, a_lam_q1, a_lam_k1, a_lam_q2,
           a_lam_k2, a_subln_g, b_w_qkv, b_w_o, b_qnorm_g, b_knorm_g, f_w_in, f_w_out, final_g):
    batch, seq, _ = x.shape
    assert seq % QKV_ROW_TILE == 0 and seq % ROW_TILE == 0
    assert seq % KV_CHUNK == 0 and seq % Q_LANES == 0

    t = np.arange(seq)
    row_pos, col_pos = t // GRID_W, t % GRID_W
    tables_a = _rope_tables([t], A_ROT // 2, ROPE_THETA_1D, A_HEAD_DIM, seq)
    tables_b = _rope_tables([row_pos, col_pos], B_HEAD_DIM // 4, ROPE_THETA_AXIAL,
                            B_HEAD_DIM, seq)
    cos_a, sin_a = _rope_angles(t, A_ROT // 2, ROPE_THETA_1D)
    tables_a_t = (_table(cos_a.T), _table(sin_a.T))
    cos_r, sin_r = _rope_angles(row_pos, B_HEAD_DIM // 4, ROPE_THETA_AXIAL)
    cos_c, sin_c = _rope_angles(col_pos, B_HEAD_DIM // 4, ROPE_THETA_AXIAL)
    q_scale_b = B_HEAD_DIM ** -0.5 * LOG2E
    tables_b_t = (_table(q_scale_b * np.concatenate([cos_r, cos_r, cos_c, cos_c], axis=1).T),
                  _table(q_scale_b * np.concatenate([-sin_r, sin_r, -sin_c, sin_c], axis=1).T))

    mod = _mod_call(c, ada_w, ada_b).reshape(DEPTH * batch, 1, 6 * D_MODEL)

    for i in range(DEPTH):
        j = i // 2
        if i % 2 == 0:
            w = a_w_qkv[j].astype(BF16)
            w_qv_t = jnp.concatenate([w[:, :D_MODEL], w[:, 2 * D_MODEL:]], axis=1).T
            q_t, k, v_t = _qkv_call(_qkv_a_kernel, "norm_qkv_rope_a", i, x, mod, norm1_g[i],
                                    w_qv_t, w[:, D_MODEL:2 * D_MODEL], [], tables_a, tables_a_t)
            o = _attn_a_call(i, q_t, k, v_t,
                             (a_lam_q1[j], a_lam_k1[j], a_lam_q2[j], a_lam_k2[j]), a_subln_g[j])
            w_o = a_w_o
        else:
            kv_w = B_N_KV * B_HEAD_DIM
            w = b_w_qkv[j].astype(BF16)
            w_qv_t = jnp.concatenate([w[:, :D_MODEL], w[:, D_MODEL + kv_w:]], axis=1).T
            q_gain = jnp.broadcast_to(b_qnorm_g[j][:, None], (B_HEAD_DIM, QKV_ROW_TILE))
            q_t, k, v_t = _qkv_call(_qkv_b_kernel, "norm_qkv_rope_b", i, x, mod, norm1_g[i],
                                    w_qv_t, w[:, D_MODEL:D_MODEL + kv_w],
                                    [q_gain, b_knorm_g[j].reshape(1, LANES)],
                                    tables_b, tables_b_t)
            o = _attn_b_call(q_t, k, v_t)
            w_o = b_w_o
        x = _post_call(i, i == DEPTH - 1, x, o, mod, w_o, j, norm2_g[i], f_w_in, f_w_out,
                       final_g)
    return x
```

```python
import functools
import math

import jax
import jax.numpy as jnp
import numpy as np
from jax.experimental import pallas as pl
from jax.experimental.pallas import tpu as pltpu

D_MODEL = 1024
DEPTH = 2
GRID_W = 64
EPS = 1e-6
LOG2E = math.log2(math.e)

A_HEAD_DIM = 64
A_N_HEADS = D_MODEL // (2 * A_HEAD_DIM)
A_ROT = A_HEAD_DIM // 4
ROPE_THETA_1D = 500000.0

B_HEAD_DIM = 128
B_N_HEADS = D_MODEL // B_HEAD_DIM
B_N_KV = max(1, B_N_HEADS // 4)
B_GROUP = B_N_HEADS // B_N_KV
ROPE_THETA_AXIAL = 10000.0

D_FF = -(-8 * D_MODEL // (3 * 256)) * 256

LANES = 128
SUBLANES = 8
VMEM_LIMIT_BYTES = 56 * 1024 * 1024

ROW_TILE = 512
QKV_ROW_TILE = 512
PROJ_WIDTH = 1024
Q_LANES = 1024
Q_TILE_A = Q_LANES // 2
Q_TILE_B = Q_LANES // B_GROUP
KV_CHUNK = 256
FF_CHUNK = 256
ONES_ROWS = 16
V_ROWS = LANES + ONES_ROWS
PIPELINE_LAG = 2

BF16 = jnp.bfloat16
F32 = jnp.float32


def _lambda_init(layer_idx):
    return 0.8 - 0.6 * math.exp(-0.3 * layer_idx)


def _rms_scale(x):
    return jax.lax.rsqrt(jnp.mean(x * x, axis=-1, keepdims=True) + EPS)


def _mod_kernel(c_ref, w_ref, b_ref, o_ref):
    c = c_ref[...]
    cond = c * jax.nn.sigmoid(c)
    y = jnp.dot(cond.astype(BF16), w_ref[0].astype(BF16), preferred_element_type=F32)
    o_ref[0] = y + b_ref[0]


def _mod_call(c, ada_w, ada_b):
    rows = c.shape[0]
    n_chunks = 6
    return pl.pallas_call(
        _mod_kernel,
        grid=(DEPTH, n_chunks),
        in_specs=[
            pl.BlockSpec((rows, D_MODEL), lambda i, j: (0, 0)),
            pl.BlockSpec((1, D_MODEL, D_MODEL), lambda i, j: (i, 0, j)),
            pl.BlockSpec((1, 1, D_MODEL), lambda i, j: (i, 0, j)),
        ],
        out_specs=pl.BlockSpec((1, rows, D_MODEL), lambda i, j: (i, 0, j)),
        out_shape=jax.ShapeDtypeStruct((DEPTH, rows, 6 * D_MODEL), F32),
        compiler_params=pltpu.CompilerParams(
            dimension_semantics=("arbitrary", "arbitrary"),
            vmem_limit_bytes=VMEM_LIMIT_BYTES),
        name="adaln_mod",
    )(c, ada_w, ada_b.reshape(DEPTH, 1, 6 * D_MODEL))


def _rope(x, cos_t, sin_fwd, sin_bwd, shift):
    fwd = pltpu.roll(x, LANES - shift, axis=1)
    bwd = pltpu.roll(x, shift, axis=1)
    return x * cos_t + fwd * sin_fwd + bwd * sin_bwd


def _modulated_norm(x, g, scale, shift):
    return (x * _rms_scale(x)) * (g * (1.0 + scale)) + shift


def _transposed_projection(w_t_ref, h, block):
    n_rows = w_t_ref.shape[0]
    h_t = h.T
    for r in range(0, n_rows, block):
        yield r, jnp.dot(w_t_ref[r:min(r + block, n_rows), :], h_t, preferred_element_type=F32)


def _store_v_heads(vt_ref, first_head, y):
    ones = jnp.ones((ONES_ROWS, y.shape[1]), BF16)
    for u in range(y.shape[0] // LANES):
        base = (first_head + u) * V_ROWS
        vt_ref[0, base:base + LANES, :] = y[u * LANES:(u + 1) * LANES].astype(BF16)
        vt_ref[0, base + LANES:base + V_ROWS, :] = ones


def _qkv_a_kernel(x_ref, g_ref, shift_ref, scale_ref, wqv_t_ref, wk_ref, cos_ref, sf_ref, sb_ref,
                  cos_t_ref, sin_t_ref, qt_ref, k_ref, vt_ref):
    h = _modulated_norm(x_ref[0], g_ref[...], scale_ref[0], shift_ref[0]).astype(BF16)

    half = A_ROT // 2
    cos_t, sf, sb = cos_ref[...], sf_ref[...], sb_ref[...]
    for c in range(0, D_MODEL, PROJ_WIDTH):
        y = jnp.dot(h, wk_ref[:, c:c + PROJ_WIDTH], preferred_element_type=F32)
        for u in range(0, PROJ_WIDTH, LANES):
            k = _rope(y[:, u:u + LANES], cos_t, sf, sb, half)
            k_ref[0, :, c + u:c + u + LANES] = k.astype(BF16)

    cos_rows, sin_rows = cos_t_ref[...], sin_t_ref[...]
    for r, y in _transposed_projection(wqv_t_ref, h, PROJ_WIDTH):
        if r < D_MODEL:
            pieces = []
            for base in range(0, PROJ_WIDTH, A_HEAD_DIM):
                lo, hi = y[base:base + half], y[base + half:base + A_ROT]
                pieces += [lo * cos_rows - hi * sin_rows, hi * cos_rows + lo * sin_rows,
                           y[base + A_ROT:base + A_HEAD_DIM]]
            q_t = jnp.concatenate(pieces, axis=0) * (A_HEAD_DIM ** -0.5 * LOG2E)
            qt_ref[0, r:r + PROJ_WIDTH, :] = q_t.astype(BF16)
        else:
            _store_v_heads(vt_ref, (r - D_MODEL) // LANES, y)


def _qkv_b_kernel(x_ref, g_ref, shift_ref, scale_ref, wqv_t_ref, wk_ref, qg_ref, kg_ref,
                  cos_ref, sf_ref, sb_ref, cos_t_ref, sin_t_ref, qt_ref, k_ref, vt_ref):
    h = _modulated_norm(x_ref[0], g_ref[...], scale_ref[0], shift_ref[0]).astype(BF16)

    quarter = B_HEAD_DIM // 4
    cos_t, sf, sb = cos_ref[...], sf_ref[...], sb_ref[...]
    y = jnp.dot(h, wk_ref[...], preferred_element_type=F32)
    for u in range(0, wk_ref.shape[1], LANES):
        k = y[:, u:u + LANES]
        k = (k * _rms_scale(k)) * kg_ref[...]
        k = _rope(k, cos_t, sf, sb, quarter)
        k_ref[0, :, u:u + LANES] = k.astype(BF16)

    def swap_pairs(a):
        return jnp.concatenate(
            [a[quarter:2 * quarter], a[:quarter], a[3 * quarter:], a[2 * quarter:3 * quarter]],
            axis=0)

    q_gain = qg_ref[...]
    gain_cos = q_gain * cos_t_ref[...]
    gain_sin = swap_pairs(q_gain) * sin_t_ref[...]
    for r, y in _transposed_projection(wqv_t_ref, h, PROJ_WIDTH):
        if r < D_MODEL:
            for base in range(0, PROJ_WIDTH, B_HEAD_DIM):
                q = y[base:base + B_HEAD_DIM]
                q = q * jax.lax.rsqrt(jnp.mean(q * q, axis=0, keepdims=True) + EPS)
                q = q * gain_cos + swap_pairs(q) * gain_sin
                qt_ref[0, r + base:r + base + B_HEAD_DIM, :] = q.astype(BF16)
        else:
            _store_v_heads(vt_ref, (r - D_MODEL) // LANES, y)


def _mod_spec(mod, layer, chunk):
    batch = mod.shape[0] // DEPTH
    return pl.BlockSpec((1, 1, D_MODEL), lambda b, i: (layer * batch + b, 0, chunk))


def _const_spec(shape):
    return pl.BlockSpec(shape, lambda *_: (0,) * len(shape), pipeline_mode=pl.Buffered(1))


def _qkv_call(kernel_fn, name, layer, x, mod, norm_g, w_qv_t, w_k, extra, tables, tables_t):
    batch, seq, _ = x.shape
    n_q = D_MODEL
    n_v = (w_qv_t.shape[0] - n_q) // LANES * V_ROWS
    n_k = w_k.shape[1]
    table_spec = pl.BlockSpec((QKV_ROW_TILE, LANES), lambda b, i: (i, 0))
    table_t_spec = pl.BlockSpec((tables_t[0].shape[0], QKV_ROW_TILE), lambda b, i: (0, i))
    in_specs = [
        pl.BlockSpec((1, QKV_ROW_TILE, D_MODEL), lambda b, i: (b, i, 0)),
        _const_spec((1, D_MODEL)),
        _mod_spec(mod, layer, 0),
        _mod_spec(mod, layer, 1),
        _const_spec(w_qv_t.shape),
        _const_spec(w_k.shape),
    ] + [_const_spec(e.shape) for e in extra] + [table_spec] * 3 + [table_t_spec] * 2
    return pl.pallas_call(
        kernel_fn,
        grid=(batch, seq // QKV_ROW_TILE),
        in_specs=in_specs,
        out_specs=[
            pl.BlockSpec((1, n_q, QKV_ROW_TILE), lambda b, i: (b, 0, i)),
            pl.BlockSpec((1, QKV_ROW_TILE, n_k), lambda b, i: (b, i, 0)),
            pl.BlockSpec((1, n_v, QKV_ROW_TILE), lambda b, i: (b, 0, i)),
        ],
        out_shape=[
            jax.ShapeDtypeStruct((batch, n_q, seq), BF16),
            jax.ShapeDtypeStruct((batch, seq, n_k), BF16),
            jax.ShapeDtypeStruct((batch, n_v, seq), BF16),
        ],
        compiler_params=pltpu.CompilerParams(
            dimension_semantics=("arbitrary", "arbitrary"),
            vmem_limit_bytes=VMEM_LIMIT_BYTES),
        name=name,
    )(x, norm_g.reshape(1, D_MODEL), mod, mod, w_qv_t, w_k, *extra, *tables, *tables_t)


def _scores_and_weighted_sum(q_t, k_ref, vt_ref, write, read, seq):
    s_w_ref, m_w_ref = write
    s_r_ref, m_r_ref = read
    n = s_w_ref.shape[1]
    groups = KV_CHUNK // SUBLANES
    m_prev = m_r_ref[0:1, :] if vt_ref is not None else None
    col_max = None
    acc = None
    for j in range(seq // KV_CHUNK):
        rows = slice(j * KV_CHUNK, (j + 1) * KV_CHUNK)
        if vt_ref is not None:
            p = jnp.exp2((s_r_ref[rows, :] - m_prev).astype(BF16))
            pv = jnp.dot(vt_ref[0, :, rows], p, preferred_element_type=F32)
            acc = pv if acc is None else acc + pv

        if q_t is not None:
            s = jnp.dot(k_ref[0, rows, :], q_t, preferred_element_type=F32)
            s_w_ref[rows, :] = s
            part = jnp.max(s.reshape(groups, SUBLANES, n), axis=0)
            col_max = part if col_max is None else jnp.maximum(col_max, part)
    if q_t is not None:
        m = jnp.max(col_max, axis=0, keepdims=True)
        m_w_ref[...] = jnp.broadcast_to(m, (SUBLANES, n))
    return acc


def _pipelined_step(n_tiles, buffers, step_fn):
    assert n_tiles % 2 == 0 and n_tiles > PIPELINE_LAG
    t = pl.program_id(0)
    even, odd = (buffers[0], buffers[1]), (buffers[1], buffers[0])
    full = jnp.logical_and(t >= PIPELINE_LAG, t < n_tiles)
    variants = [
        (t == 0, even, (True, False, False)),
        (t == 1, odd, (True, True, False)),
        (jnp.logical_and(full, t % 2 == 0), even, (True, True, True)),
        (jnp.logical_and(full, t % 2 == 1), odd, (True, True, True)),
        (t == n_tiles, even, (False, True, True)),
        (t == n_tiles + 1, odd, (False, False, True)),
    ]
    for condition, (write, read), stages in variants:
        pl.when(condition)(functools.partial(step_fn, write, read, *stages))


def _normalised(acc, cols):
    return acc[:LANES, cols] / acc[LANES:LANES + 1, cols]


def _attn_a_kernel(lam_init, n_tiles, q_ref, k_ref, vt_ref, lq1_ref, lk1_ref, lq2_ref, lk2_ref,
                   g_ref, o_ref, *scratch):
    seq = k_ref.shape[1]
    tq = q_ref.shape[2]

    def step(write, read, scores, weighted, finish):
        if finish:
            acc_done = read[2][...]
            lam = (jnp.exp(jnp.sum(lq1_ref[...] * lk1_ref[...], axis=1, keepdims=True))
                   - jnp.exp(jnp.sum(lq2_ref[...] * lk2_ref[...], axis=1, keepdims=True))
                   + lam_init)
            o_t = (_normalised(acc_done, slice(0, tq))
                   - lam * _normalised(acc_done, slice(tq, 2 * tq)))
            o = o_t.T
            o = (o * _rms_scale(o)) * g_ref[...]
            o_ref[0] = (o * (1.0 - lam_init)).astype(BF16)

        q_t = None
        if scores:
            q = q_ref[0]
            row = jax.lax.broadcasted_iota(jnp.int32, q.shape, 0)
            zero = jnp.zeros_like(q)
            q_t = jnp.concatenate([jnp.where(row < A_HEAD_DIM, q, zero),
                                   jnp.where(row >= A_HEAD_DIM, q, zero)], axis=1)
        if scores or weighted:
            acc = _scores_and_weighted_sum(q_t, k_ref, vt_ref if weighted else None,
                                           write[:2], read[:2], seq)
            if weighted:
                write[2][...] = acc

    _pipelined_step(n_tiles, (scratch[:3], scratch[3:]), step)


def _attn_b_kernel(n_tiles, q_ref, k_ref, vt_ref, o_ref, *scratch):
    seq = k_ref.shape[1]
    tq = q_ref.shape[2]

    def step(write, read, scores, weighted, finish):
        if finish:
            acc_done = read[2][...]
            for g in range(B_GROUP):
                o_t = _normalised(acc_done, slice(g * tq, (g + 1) * tq))
                o_ref[0, :, g * LANES:(g + 1) * LANES] = o_t.T.astype(BF16)

        q_t = None
        if scores:
            q = q_ref[0]
            q_t = jnp.concatenate([q[g * LANES:(g + 1) * LANES, :] for g in range(B_GROUP)],
                                  axis=1)
        if scores or weighted:
            acc = _scores_and_weighted_sum(q_t, k_ref, vt_ref if weighted else None,
                                           write[:2], read[:2], seq)
            if weighted:
                write[2][...] = acc

    _pipelined_step(n_tiles, (scratch[:3], scratch[3:]), step)


def _attn_scratch(seq):
    buffer_set = [pltpu.VMEM((seq, Q_LANES), F32),
                  pltpu.VMEM((SUBLANES, Q_LANES), F32),
                  pltpu.VMEM((V_ROWS, Q_LANES), F32)]
    return buffer_set + buffer_set


def _attn_call(kernel_fn, name, q_t, k, v_t, n_heads, q_rows, tile, extra):
    batch, seq, _ = k.shape
    tiles_per_head = seq // tile
    n_tiles = batch * n_heads * tiles_per_head

    def split(t):
        t = jnp.clip(t, 0, n_tiles - 1)
        return (t // (n_heads * tiles_per_head), (t // tiles_per_head) % n_heads,
                t % tiles_per_head)

    def q_map(t):
        b, h, i = split(t)
        return (b, h, i)

    def k_map(t):
        b, h, _ = split(t)
        return (b, 0, h)

    def v_map(t):
        b, h, _ = split(t - 1)
        return (b, h, 0)

    def o_map(t):
        b, h, i = split(t - PIPELINE_LAG)
        return (b, i, h)

    return pl.pallas_call(
        functools.partial(kernel_fn, n_tiles),
        grid=(n_tiles + PIPELINE_LAG,),
        in_specs=[pl.BlockSpec((1, q_rows, tile), q_map),
                  pl.BlockSpec((1, seq, LANES), k_map),
                  pl.BlockSpec((1, V_ROWS, seq), v_map)] + [_const_spec(e.shape) for e in extra],
        out_specs=pl.BlockSpec((1, tile, D_MODEL // n_heads), o_map),
        out_shape=jax.ShapeDtypeStruct((batch, seq, D_MODEL), BF16),
        scratch_shapes=_attn_scratch(seq),
        compiler_params=pltpu.CompilerParams(
            dimension_semantics=("arbitrary",),
            vmem_limit_bytes=VMEM_LIMIT_BYTES),
        name=name,
    )(q_t, k, v_t, *extra)


def _attn_a_call(layer, q_t, k, v_t, lam_params, subln_g):
    extra = [p.reshape(1, A_HEAD_DIM) for p in lam_params] + [subln_g.reshape(1, LANES)]
    return _attn_call(functools.partial(_attn_a_kernel, _lambda_init(layer)), "diff_attention",
                      q_t, k, v_t, A_N_HEADS, LANES, Q_TILE_A, extra)


def _attn_b_call(q_t, k, v_t):
    return _attn_call(_attn_b_kernel, "axial_gqa_attention", q_t, k, v_t, B_N_KV,
                      B_GROUP * B_HEAD_DIM, Q_TILE_B, [])


def _post_kernel(final_norm, layer, w_o_layer, x_ref, o_ref, wo_hbm, gate1_ref, g2_ref,
                 shift2_ref, scale2_ref, gate2_ref, win_hbm, wout_hbm, fg_ref, out_ref,
                 act_ref, wo_ref, win_ref, wout_ref, sems):
    n_chunks = D_FF // FF_CHUNK

    def cols(c, up):
        start = (D_FF if up else 0) + c * FF_CHUNK
        return slice(start, start + FF_CHUNK)

    pieces = [(wo_hbm.at[w_o_layer], wo_ref)]
    for c in range(n_chunks):
        for up in (False, True):
            pieces.append((win_hbm.at[layer, :, cols(c, up)], win_ref.at[:, cols(c, up)]))
    pieces.append((wout_hbm.at[layer], wout_ref))
    copies = [pltpu.make_async_copy(src, dst, sems.at[i]) for i, (src, dst) in enumerate(pieces)]

    def step(arrived):
        arrived(0)
        y = jnp.dot(o_ref[0], wo_ref[...].astype(BF16), preferred_element_type=F32)
        x1 = x_ref[0] + gate1_ref[0] * y
        out_ref[0] = x1
        h = _modulated_norm(x1, g2_ref[...], scale2_ref[0], shift2_ref[0]).astype(BF16)
        for c in range(n_chunks):
            arrived(1 + 2 * c)
            arrived(2 + 2 * c)
            gate = jnp.dot(h, win_ref[:, cols(c, False)].astype(BF16), preferred_element_type=F32)
            up = jnp.dot(h, win_ref[:, cols(c, True)].astype(BF16), preferred_element_type=F32)
            act_ref[:, cols(c, False)] = ((gate * jax.nn.sigmoid(gate)) * up).astype(BF16)
        arrived(len(copies) - 1)
        f = jnp.dot(act_ref[...], wout_ref[...].astype(BF16), preferred_element_type=F32)
        x2 = out_ref[0] + gate2_ref[0] * f
        if final_norm:
            x2 = (x2 * _rms_scale(x2)) * fg_ref[...]
        out_ref[0] = x2

    first = jnp.logical_and(pl.program_id(0) == 0, pl.program_id(1) == 0)

    @pl.when(first)
    def _():
        for copy in copies:
            copy.start()
        step(lambda i: copies[i].wait())

    @pl.when(jnp.logical_not(first))
    def _():
        step(lambda i: None)


def _post_call(layer, final_norm, x, o, mod, w_o, w_o_layer, norm2_g, w_in, w_out, final_g):
    batch, seq, _ = x.shape
    row_spec = pl.BlockSpec((1, ROW_TILE, D_MODEL), lambda b, i: (b, i, 0))
    hbm_spec = pl.BlockSpec(memory_space=pl.ANY)
    return pl.pallas_call(
        functools.partial(_post_kernel, final_norm, layer, w_o_layer),
        grid=(batch, seq // ROW_TILE),
        in_specs=[
            row_spec,
            row_spec,
            hbm_spec,
            _mod_spec(mod, layer, 2),
            _const_spec((1, D_MODEL)),
            _mod_spec(mod, layer, 3),
            _mod_spec(mod, layer, 4),
            _mod_spec(mod, layer, 5),
            hbm_spec,
            hbm_spec,
            _const_spec((1, D_MODEL)),
        ],
        out_specs=row_spec,
        out_shape=jax.ShapeDtypeStruct((batch, seq, D_MODEL), F32),
        scratch_shapes=[pltpu.VMEM((ROW_TILE, D_FF), BF16),
                        pltpu.VMEM(w_o.shape[1:], w_o.dtype),
                        pltpu.VMEM(w_in.shape[1:], w_in.dtype),
                        pltpu.VMEM(w_out.shape[1:], w_out.dtype),
                        pltpu.SemaphoreType.DMA((2 * (D_FF // FF_CHUNK) + 2,))],
        compiler_params=pltpu.CompilerParams(
            dimension_semantics=("arbitrary", "arbitrary"),
            vmem_limit_bytes=VMEM_LIMIT_BYTES),
        name="wo_ffn",
    )(x, o, w_o, mod, norm2_g.reshape(1, D_MODEL), mod, mod, mod, w_in, w_out,
      final_g.reshape(1, D_MODEL))


def _rope_angles(pos, half, theta):
    freqs = float(theta) ** (-np.arange(half, dtype=np.float64) / half)
    ang = np.asarray(pos, np.float64)[:, None] * freqs[None, :]
    return np.cos(ang), np.sin(ang)


def _rope_tables(pos_per_lane_group, half, theta, period, seq):
    cos_cols, sf_cols, sb_cols = [], [], []
    for pos in pos_per_lane_group:
        c, s = _rope_angles(pos, half, theta)
        z = np.zeros_like(s)
        cos_cols += [c, c]
        sf_cols += [-s, z]
        sb_cols += [z, s]
    rest = period - 2 * half * len(pos_per_lane_group)
    if rest:
        cos_cols.append(np.ones((seq, rest)))
        sf_cols.append(np.zeros((seq, rest)))
        sb_cols.append(np.zeros((seq, rest)))
    reps = LANES // period
    return tuple(_table(np.tile(np.concatenate(cols, axis=1), (1, reps)))
                 for cols in (cos_cols, sf_cols, sb_cols))


def _table(values):
    return jnp.asarray(np.ascontiguousarray(values, dtype=np.float32))


def kernel(x, c, ada_w, ada_b, norm1_g, norm2_g, a_w_qkv, a_w_o, a_lam_q1, a_lam_k1, a_lam_q2,
           a_lam_k2, a_subln_g, b_w_qkv, b_w_o, b_qnorm_g, b_knorm_g, f_w_in, f_w_out, final_g):
    batch, seq, _ = x.shape
    assert seq % QKV_ROW_TILE == 0 and seq % ROW_TILE == 0
    assert seq % KV_CHUNK == 0 and seq % Q_LANES == 0

    t = np.arange(seq)
    row_pos, col_pos = t // GRID_W, t % GRID_W
    tables_a = _rope_tables([t], A_ROT // 2, ROPE_THETA_1D, A_HEAD_DIM, seq)
    tables_b = _rope_tables([row_pos, col_pos], B_HEAD_DIM // 4, ROPE_THETA_AXIAL,
                            B_HEAD_DIM, seq)
    cos_a, sin_a = _rope_angles(t, A_ROT // 2, ROPE_THETA_1D)
    tables_a_t = (_table(cos_a.T), _table(sin_a.T))
    cos_r, sin_r = _rope_angles(row_pos, B_HEAD_DIM // 4, ROPE_THETA_AXIAL)
    cos_c, sin_c = _rope_angles(col_pos, B_HEAD_DIM // 4, ROPE_THETA_AXIAL)
    q_scale_b = B_HEAD_DIM ** -0.5 * LOG2E
    tables_b_t = (_table(q_scale_b * np.concatenate([cos_r, cos_r, cos_c, cos_c], axis=1).T),
                  _table(q_scale_b * np.concatenate([-sin_r, sin_r, -sin_c, sin_c], axis=1).T))

    mod = _mod_call(c, ada_w, ada_b).reshape(DEPTH * batch, 1, 6 * D_MODEL)

    for i in range(DEPTH):
        j = i // 2
        if i % 2 == 0:
            w = a_w_qkv[j].astype(BF16)
            w_qv_t = jnp.concatenate([w[:, :D_MODEL], w[:, 2 * D_MODEL:]], axis=1).T
            q_t, k, v_t = _qkv_call(_qkv_a_kernel, "norm_qkv_rope_a", i, x, mod, norm1_g[i],
                                    w_qv_t, w[:, D_MODEL:2 * D_MODEL], [], tables_a, tables_a_t)
            o = _attn_a_call(i, q_t, k, v_t,
                             (a_lam_q1[j], a_lam_k1[j], a_lam_q2[j], a_lam_k2[j]), a_subln_g[j])
            w_o = a_w_o
        else:
            kv_w = B_N_KV * B_HEAD_DIM
            w = b_w_qkv[j].astype(BF16)
            w_qv_t = jnp.concatenate([w[:, :D_MODEL], w[:, D_MODEL + kv_w:]], axis=1).T
            q_gain = jnp.broadcast_to(b_qnorm_g[j][:, None], (B_HEAD_DIM, QKV_ROW_TILE))
            q_t, k, v_t = _qkv_call(_qkv_b_kernel, "norm_qkv_rope_b", i, x, mod, norm1_g[i],
                                    w_qv_t, w[:, D_MODEL:D_MODEL + kv_w],
                                    [q_gain, b_knorm_g[j].reshape(1, LANES)],
                                    tables_b, tables_b_t)
            o = _attn_b_call(q_t, k, v_t)
            w_o = b_w_o
        x = _post_call(i, i == DEPTH - 1, x, o, mod, w_o, j, norm2_g[i], f_w_in, f_w_out,
                       final_g)
    return x
```

```python
import functools
import math

import jax
import jax.numpy as jnp
import numpy as np
from jax.experimental import pallas as pl
from jax.experimental.pallas import tpu as pltpu

D_MODEL = 1024
DEPTH = 2
GRID_W = 64
EPS = 1e-6
LOG2E = math.log2(math.e)

A_HEAD_DIM = 64
A_N_HEADS = D_MODEL // (2 * A_HEAD_DIM)
A_ROT = A_HEAD_DIM // 4
ROPE_THETA_1D = 500000.0

B_HEAD_DIM = 128
B_N_HEADS = D_MODEL // B_HEAD_DIM
B_N_KV = max(1, B_N_HEADS // 4)
B_GROUP = B_N_HEADS // B_N_KV
ROPE_THETA_AXIAL = 10000.0

D_FF = -(-8 * D_MODEL // (3 * 256)) * 256

LANES = 128
SUBLANES = 8
VMEM_LIMIT_BYTES = 56 * 1024 * 1024

ROW_TILE = 512
QKV_ROW_TILE = 512
PROJ_WIDTH = 1024
Q_LANES = 1024
Q_TILE_A = Q_LANES // 2
Q_TILE_B = Q_LANES // B_GROUP
KV_CHUNK = 256
FF_CHUNK = 256
ONES_ROWS = 16
V_ROWS = LANES + ONES_ROWS
PIPELINE_LAG = 2

BF16 = jnp.bfloat16
F32 = jnp.float32


def _lambda_init(layer_idx):
    return 0.8 - 0.6 * math.exp(-0.3 * layer_idx)


def _rms_scale(x):
    return jax.lax.rsqrt(jnp.mean(x * x, axis=-1, keepdims=True) + EPS)


def _mod_kernel(c_ref, w_ref, b_ref, o_ref):
    c = c_ref[...]
    cond = c * jax.nn.sigmoid(c)
    y = jnp.dot(cond.astype(BF16), w_ref[0].astype(BF16), preferred_element_type=F32)
    o_ref[0] = y + b_ref[0]


def _mod_call(c, ada_w, ada_b):
    rows = c.shape[0]
    n_chunks = 6
    return pl.pallas_call(
        _mod_kernel,
        grid=(DEPTH, n_chunks),
        in_specs=[
            pl.BlockSpec((rows, D_MODEL), lambda i, j: (0, 0)),
            pl.BlockSpec((1, D_MODEL, D_MODEL), lambda i, j: (i, 0, j)),
            pl.BlockSpec((1, 1, D_MODEL), lambda i, j: (i, 0, j)),
        ],
        out_specs=pl.BlockSpec((1, rows, D_MODEL), lambda i, j: (i, 0, j)),
        out_shape=jax.ShapeDtypeStruct((DEPTH, rows, 6 * D_MODEL), F32),
        compiler_params=pltpu.CompilerParams(
            dimension_semantics=("arbitrary", "arbitrary"),
            vmem_limit_bytes=VMEM_LIMIT_BYTES),
        name="adaln_mod",
    )(c, ada_w, ada_b.reshape(DEPTH, 1, 6 * D_MODEL))


def _rope(x, cos_t, sin_fwd, sin_bwd, shift):
    fwd = pltpu.roll(x, LANES - shift, axis=1)
    bwd = pltpu.roll(x, shift, axis=1)
    return x * cos_t + fwd * sin_fwd + bwd * sin_bwd


def _modulated_norm(x, g, scale, shift):
    return (x * _rms_scale(x)) * (g * (1.0 + scale)) + shift


def _transposed_projection(w_t_ref, h, block):
    n_rows = w_t_ref.shape[0]
    h_t = h.T
    for r in range(0, n_rows, block):
        yield r, jnp.dot(w_t_ref[r:min(r + block, n_rows), :], h_t, preferred_element_type=F32)


def _store_v_heads(vt_ref, first_head, y):
    ones = jnp.ones((ONES_ROWS, y.shape[1]), BF16)
    for u in range(y.shape[0] // LANES):
        base = (first_head + u) * V_ROWS
        vt_ref[0, base:base + LANES, :] = y[u * LANES:(u + 1) * LANES].astype(BF16)
        vt_ref[0, base + LANES:base + V_ROWS, :] = ones


def _qkv_a_kernel(x_ref, g_ref, shift_ref, scale_ref, wqv_t_ref, wk_ref, cos_ref, sf_ref, sb_ref,
                  cos_t_ref, sin_t_ref, qt_ref, k_ref, vt_ref):
    h = _modulated_norm(x_ref[0], g_ref[...], scale_ref[0], shift_ref[0]).astype(BF16)

    half = A_ROT // 2
    cos_t, sf, sb = cos_ref[...], sf_ref[...], sb_ref[...]
    for c in range(0, D_MODEL, PROJ_WIDTH):
        y = jnp.dot(h, wk_ref[:, c:c + PROJ_WIDTH], preferred_element_type=F32)
        for u in range(0, PROJ_WIDTH, LANES):
            k = _rope(y[:, u:u + LANES], cos_t, sf, sb, half)
            k_ref[0, :, c + u:c + u + LANES] = k.astype(BF16)

    cos_rows, sin_rows = cos_t_ref[...], sin_t_ref[...]
    for r, y in _transposed_projection(wqv_t_ref, h, PROJ_WIDTH):
        if r < D_MODEL:
            pieces = []
            for base in range(0, PROJ_WIDTH, A_HEAD_DIM):
                lo, hi = y[base:base + half], y[base + half:base + A_ROT]
                pieces += [lo * cos_rows - hi * sin_rows, hi * cos_rows + lo * sin_rows,
                           y[base + A_ROT:base + A_HEAD_DIM]]
            q_t = jnp.concatenate(pieces, axis=0) * (A_HEAD_DIM ** -0.5 * LOG2E)
            qt_ref[0, r:r + PROJ_WIDTH, :] = q_t.astype(BF16)
        else:
            _store_v_heads(vt_ref, (r - D_MODEL) // LANES, y)


def _qkv_b_kernel(x_ref, g_ref, shift_ref, scale_ref, wqv_t_ref, wk_ref, qg_ref, kg_ref,
                  cos_ref, sf_ref, sb_ref, cos_t_ref, sin_t_ref, qt_ref, k_ref, vt_ref):
    h = _modulated_norm(x_ref[0], g_ref[...], scale_ref[0], shift_ref[0]).astype(BF16)

    quarter = B_HEAD_DIM // 4
    cos_t, sf, sb = cos_ref[...], sf_ref[...], sb_ref[...]
    y = jnp.dot(h, wk_ref[...], preferred_element_type=F32)
    for u in range(0, wk_ref.shape[1], LANES):
        k = y[:, u:u + LANES]
        k = (k * _rms_scale(k)) * kg_ref[...]
        k = _rope(k, cos_t, sf, sb, quarter)
        k_ref[0, :, u:u + LANES] = k.astype(BF16)

    def swap_pairs(a):
        return jnp.concatenate(
            [a[quarter:2 * quarter], a[:quarter], a[3 * quarter:], a[2 * quarter:3 * quarter]],
            axis=0)

    q_gain = qg_ref[...]
    gain_cos = q_gain * cos_t_ref[...]
    gain_sin = swap_pairs(q_gain) * sin_t_ref[...]
    for r, y in _transposed_projection(wqv_t_ref, h, PROJ_WIDTH):
        if r < D_MODEL:
            for base in range(0, PROJ_WIDTH, B_HEAD_DIM):
                q = y[base:base + B_HEAD_DIM]
                q = q * jax.lax.rsqrt(jnp.mean(q * q, axis=0, keepdims=True) + EPS)
                q = q * gain_cos + swap_pairs(q) * gain_sin
                qt_ref[0, r + base:r + base + B_HEAD_DIM, :] = q.astype(BF16)
        else:
            _store_v_heads(vt_ref, (r - D_MODEL) // LANES, y)


def _mod_spec(mod, layer, chunk):
    batch = mod.shape[0] // DEPTH
    return pl.BlockSpec((1, 1, D_MODEL), lambda b, i: (layer * batch + b, 0, chunk))


def _const_spec(shape):
    return pl.BlockSpec(shape, lambda *_: (0,) * len(shape), pipeline_mode=pl.Buffered(1))


def _qkv_call(kernel_fn, name, layer, x, mod, norm_g, w_qv_t, w_k, extra, tables, tables_t):
    batch, seq, _ = x.shape
    n_q = D_MODEL
    n_v = (w_qv_t.shape[0] - n_q) // LANES * V_ROWS
    n_k = w_k.shape[1]
    table_spec = pl.BlockSpec((QKV_ROW_TILE, LANES), lambda b, i: (i, 0))
    table_t_spec = pl.BlockSpec((tables_t[0].shape[0], QKV_ROW_TILE), lambda b, i: (0, i))
    in_specs = [
        pl.BlockSpec((1, QKV_ROW_TILE, D_MODEL), lambda b, i: (b, i, 0)),
        _const_spec((1, D_MODEL)),
        _mod_spec(mod, layer, 0),
        _mod_spec(mod, layer, 1),
        _const_spec(w_qv_t.shape),
        _const_spec(w_k.shape),
    ] + [_const_spec(e.shape) for e in extra] + [table_spec] * 3 + [table_t_spec] * 2
    return pl.pallas_call(
        kernel_fn,
        grid=(batch, seq // QKV_ROW_TILE),
        in_specs=in_specs,
        out_specs=[
            pl.BlockSpec((1, n_q, QKV_ROW_TILE), lambda b, i: (b, 0, i)),
            pl.BlockSpec((1, QKV_ROW_TILE, n_k), lambda b, i: (b, i, 0)),
            pl.BlockSpec((1, n_v, QKV_ROW_TILE), lambda b, i: (b, 0, i)),
        ],
        out_shape=[
            jax.ShapeDtypeStruct((batch, n_q, seq), BF16),
            jax.ShapeDtypeStruct((batch, seq, n_k), BF16),
            jax.ShapeDtypeStruct((batch, n_v, seq), BF16),
        ],
        compiler_params=pltpu.CompilerParams(
            dimension_semantics=("arbitrary", "arbitrary"),
            vmem_limit_bytes=VMEM_LIMIT_BYTES),
        name=name,
    )(x, norm_g.reshape(1, D_MODEL), mod, mod, w_qv_t, w_k, *extra, *tables, *tables_t)


def _scores_and_weighted_sum(q_t, k_ref, vt_ref, write, read, p_refs, seq):
    s_w_ref, m_w_ref = write
    s_r_ref, m_r_ref = read
    n = s_w_ref.shape[1]
    groups = KV_CHUNK // SUBLANES
    m_prev = m_r_ref[0:1, :] if vt_ref is not None else None
    col_max = None
    acc = None
    for j in range(seq // KV_CHUNK):
        rows = slice(j * KV_CHUNK, (j + 1) * KV_CHUNK)
        if vt_ref is not None:
            p_ref = p_refs[j % 2]
            p_ref[...] = jnp.exp2(s_r_ref[rows, :] - m_prev).astype(BF16)
            pv = jnp.dot(vt_ref[0, :, rows], p_ref[...], preferred_element_type=F32)
            acc = pv if acc is None else acc + pv

        if q_t is not None:
            s = jnp.dot(k_ref[0, rows, :], q_t, preferred_element_type=F32)
            s_w_ref[rows, :] = s
            part = jnp.max(s.reshape(groups, SUBLANES, n), axis=0)
            col_max = part if col_max is None else jnp.maximum(col_max, part)
    if q_t is not None:
        m = jnp.max(col_max, axis=0, keepdims=True)
        m_w_ref[...] = jnp.broadcast_to(m, (SUBLANES, n))
    return acc


def _pipelined_step(n_tiles, buffers, step_fn):
    assert n_tiles % 2 == 0 and n_tiles > PIPELINE_LAG
    t = pl.program_id(0)
    even, odd = (buffers[0], buffers[1]), (buffers[1], buffers[0])
    full = jnp.logical_and(t >= PIPELINE_LAG, t < n_tiles)
    variants = [
        (t == 0, even, (True, False, False)),
        (t == 1, odd, (True, True, False)),
        (jnp.logical_and(full, t % 2 == 0), even, (True, True, True)),
        (jnp.logical_and(full, t % 2 == 1), odd, (True, True, True)),
        (t == n_tiles, even, (False, True, True)),
        (t == n_tiles + 1, odd, (False, False, True)),
    ]
    for condition, (write, read), stages in variants:
        pl.when(condition)(functools.partial(step_fn, write, read, *stages))


def _normalised(acc, cols):
    return acc[:LANES, cols] / acc[LANES:LANES + 1, cols]


def _attn_a_kernel(lam_init, n_tiles, q_ref, k_ref, vt_ref, lq1_ref, lk1_ref, lq2_ref, lk2_ref,
                   g_ref, o_ref, *scratch):
    seq = k_ref.shape[1]
    tq = q_ref.shape[2]

    def step(write, read, scores, weighted, finish):
        if finish:
            acc_done = read[2][...]
            lam = (jnp.exp(jnp.sum(lq1_ref[...] * lk1_ref[...], axis=1, keepdims=True))
                   - jnp.exp(jnp.sum(lq2_ref[...] * lk2_ref[...], axis=1, keepdims=True))
                   + lam_init)
            o_t = (_normalised(acc_done, slice(0, tq))
                   - lam * _normalised(acc_done, slice(tq, 2 * tq)))
            o = o_t.T
            o = (o * _rms_scale(o)) * g_ref[...]
            o_ref[0] = (o * (1.0 - lam_init)).astype(BF16)

        q_t = None
        if scores:
            q = q_ref[0]
            row = jax.lax.broadcasted_iota(jnp.int32, q.shape, 0)
            zero = jnp.zeros_like(q)
            q_t = jnp.concatenate([jnp.where(row < A_HEAD_DIM, q, zero),
                                   jnp.where(row >= A_HEAD_DIM, q, zero)], axis=1)
        if scores or weighted:
            acc = _scores_and_weighted_sum(q_t, k_ref, vt_ref if weighted else None,
                                           write[:2], read[:2], scratch[6:], seq)
            if weighted:
                write[2][...] = acc

    _pipelined_step(n_tiles, (scratch[:3], scratch[3:6]), step)


def _attn_b_kernel(n_tiles, q_ref, k_ref, vt_ref, o_ref, *scratch):
    seq = k_ref.shape[1]
    tq = q_ref.shape[2]

    def step(write, read, scores, weighted, finish):
        if finish:
            acc_done = read[2][...]
            for g in range(B_GROUP):
                o_t = _normalised(acc_done, slice(g * tq, (g + 1) * tq))
                o_ref[0, :, g * LANES:(g + 1) * LANES] = o_t.T.astype(BF16)

        q_t = None
        if scores:
            q = q_ref[0]
            q_t = jnp.concatenate([q[g * LANES:(g + 1) * LANES, :] for g in range(B_GROUP)],
                                  axis=1)
        if scores or weighted:
            acc = _scores_and_weighted_sum(q_t, k_ref, vt_ref if weighted else None,
                                           write[:2], read[:2], scratch[6:], seq)
            if weighted:
                write[2][...] = acc

    _pipelined_step(n_tiles, (scratch[:3], scratch[3:6]), step)


def _attn_scratch(seq):
    buffer_set = [pltpu.VMEM((seq, Q_LANES), F32),
                  pltpu.VMEM((SUBLANES, Q_LANES), F32),
                  pltpu.VMEM((V_ROWS, Q_LANES), F32)]
    p_stage = [pltpu.VMEM((KV_CHUNK, Q_LANES), BF16)] * 2
    return buffer_set + buffer_set + p_stage


def _attn_call(kernel_fn, name, q_t, k, v_t, n_heads, q_rows, tile, extra):
    batch, seq, _ = k.shape
    tiles_per_head = seq // tile
    n_tiles = batch * n_heads * tiles_per_head

    def split(t):
        t = jnp.clip(t, 0, n_tiles - 1)
        return (t // (n_heads * tiles_per_head), (t // tiles_per_head) % n_heads,
                t % tiles_per_head)

    def q_map(t):
        b, h, i = split(t)
        return (b, h, i)

    def k_map(t):
        b, h, _ = split(t)
        return (b, 0, h)

    def v_map(t):
        b, h, _ = split(t - 1)
        return (b, h, 0)

    def o_map(t):
        b, h, i = split(t - PIPELINE_LAG)
        return (b, i, h)

    return pl.pallas_call(
        functools.partial(kernel_fn, n_tiles),
        grid=(n_tiles + PIPELINE_LAG,),
        in_specs=[pl.BlockSpec((1, q_rows, tile), q_map),
                  pl.BlockSpec((1, seq, LANES), k_map),
                  pl.BlockSpec((1, V_ROWS, seq), v_map)] + [_const_spec(e.shape) for e in extra],
        out_specs=pl.BlockSpec((1, tile, D_MODEL // n_heads), o_map),
        out_shape=jax.ShapeDtypeStruct((batch, seq, D_MODEL), BF16),
        scratch_shapes=_attn_scratch(seq),
        compiler_params=pltpu.CompilerParams(
            dimension_semantics=("arbitrary",),
            vmem_limit_bytes=VMEM_LIMIT_BYTES),
        name=name,
    )(q_t, k, v_t, *extra)


def _attn_a_call(layer, q_t, k, v_t, lam_params, subln_g):
    extra = [p.reshape(1, A_HEAD_DIM) for p in lam_params] + [subln_g.reshape(1, LANES)]
    return _attn_call(functools.partial(_attn_a_kernel, _lambda_init(layer)), "diff_attention",
                      q_t, k, v_t, A_N_HEADS, LANES, Q_TILE_A, extra)


def _attn_b_call(q_t, k, v_t):
    return _attn_call(_attn_b_kernel, "axial_gqa_attention", q_t, k, v_t, B_N_KV,
                      B_GROUP * B_HEAD_DIM, Q_TILE_B, [])


def _post_kernel(final_norm, layer, w_o_layer, x_ref, o_ref, wo_hbm, gate1_ref, g2_ref,
                 shift2_ref, scale2_ref, gate2_ref, win_hbm, wout_hbm, fg_ref, out_ref,
                 act_ref, wo_ref, win_ref, wout_ref, sems):
    n_chunks = D_FF // FF_CHUNK

    def cols(c, up):
        start = (D_FF if up else 0) + c * FF_CHUNK
        return slice(start, start + FF_CHUNK)

    pieces = [(wo_hbm.at[w_o_layer], wo_ref)]
    for c in range(n_chunks):
        for up in (False, True):
            pieces.append((win_hbm.at[layer, :, cols(c, up)], win_ref.at[:, cols(c, up)]))
    pieces.append((wout_hbm.at[layer], wout_ref))
    copies = [pltpu.make_async_copy(src, dst, sems.at[i]) for i, (src, dst) in enumerate(pieces)]

    def step(arrived):
        arrived(0)
        y = jnp.dot(o_ref[0], wo_ref[...].astype(BF16), preferred_element_type=F32)
        x1 = x_ref[0] + gate1_ref[0] * y
        out_ref[0] = x1
        h = _modulated_norm(x1, g2_ref[...], scale2_ref[0], shift2_ref[0]).astype(BF16)
        for c in range(n_chunks):
            arrived(1 + 2 * c)
            arrived(2 + 2 * c)
            gate = jnp.dot(h, win_ref[:, cols(c, False)].astype(BF16), preferred_element_type=F32)
            up = jnp.dot(h, win_ref[:, cols(c, True)].astype(BF16), preferred_element_type=F32)
            act_ref[:, cols(c, False)] = ((gate * jax.nn.sigmoid(gate)) * up).astype(BF16)
        arrived(len(copies) - 1)
        f = jnp.dot(act_ref[...], wout_ref[...].astype(BF16), preferred_element_type=F32)
        x2 = out_ref[0] + gate2_ref[0] * f
        if final_norm:
            x2 = (x2 * _rms_scale(x2)) * fg_ref[...]
        out_ref[0] = x2

    first = jnp.logical_and(pl.program_id(0) == 0, pl.program_id(1) == 0)

    @pl.when(first)
    def _():
        for copy in copies:
            copy.start()
        step(lambda i: copies[i].wait())

    @pl.when(jnp.logical_not(first))
    def _():
        step(lambda i: None)


def _post_call(layer, final_norm, x, o, mod, w_o, w_o_layer, norm2_g, w_in, w_out, final_g):
    batch, seq, _ = x.shape
    row_spec = pl.BlockSpec((1, ROW_TILE, D_MODEL), lambda b, i: (b, i, 0))
    hbm_spec = pl.BlockSpec(memory_space=pl.ANY)
    return pl.pallas_call(
        functools.partial(_post_kernel, final_norm, layer, w_o_layer),
        grid=(batch, seq // ROW_TILE),
        in_specs=[
            row_spec,
            row_spec,
            hbm_spec,
            _mod_spec(mod, layer, 2),
            _const_spec((1, D_MODEL)),
            _mod_spec(mod, layer, 3),
            _mod_spec(mod, layer, 4),
            _mod_spec(mod, layer, 5),
            hbm_spec,
            hbm_spec,
            _const_spec((1, D_MODEL)),
        ],
        out_specs=row_spec,
        out_shape=jax.ShapeDtypeStruct((batch, seq, D_MODEL), F32),
        scratch_shapes=[pltpu.VMEM((ROW_TILE, D_FF), BF16),
                        pltpu.VMEM(w_o.shape[1:], w_o.dtype),
                        pltpu.VMEM(w_in.shape[1:], w_in.dtype),
                        pltpu.VMEM(w_out.shape[1:], w_out.dtype),
                        pltpu.SemaphoreType.DMA((2 * (D_FF // FF_CHUNK) + 2,))],
        compiler_params=pltpu.CompilerParams(
            dimension_semantics=("arbitrary", "arbitrary"),
            vmem_limit_bytes=VMEM_LIMIT_BYTES),
        name="wo_ffn",
    )(x, o, w_o, mod, norm2_g.reshape(1, D_MODEL), mod, mod, mod, w_in, w_out,
      final_g.reshape(1, D_MODEL))


def _rope_angles(pos, half, theta):
    freqs = float(theta) ** (-np.arange(half, dtype=np.float64) / half)
    ang = np.asarray(pos, np.float64)[:, None] * freqs[None, :]
    return np.cos(ang), np.sin(ang)


def _rope_tables(pos_per_lane_group, half, theta, period, seq):
    cos_cols, sf_cols, sb_cols = [], [], []
    for pos in pos_per_lane_group:
        c, s = _rope_angles(pos, half, theta)
        z = np.zeros_like(s)
        cos_cols += [c, c]
        sf_cols += [-s, z]
        sb_cols += [z, s]
    rest = period - 2 * half * len(pos_per_lane_group)
    if rest:
        cos_cols.append(np.ones((seq, rest)))
        sf_cols.append(np.zeros((seq, rest)))
        sb_cols.append(np.zeros((seq, rest)))
    reps = LANES // period
    return tuple(_table(np.tile(np.concatenate(cols, axis=1), (1, reps)))
                 for cols in (cos_cols, sf_cols, sb_cols))


def _table(values):
    return jnp.asarray(np.ascontiguousarray(values, dtype=np.float32))


def kernel(x, c, ada_w, ada_b, norm1_g, norm2_g, a_w_qkv, a_w_o, a_lam_q1, a_lam_k1, a_lam_q2,
           a_lam_k2, a_subln_g, b_w_qkv, b_w_o, b_qnorm_g, b_knorm_g, f_w_in, f_w_out, final_g):
    batch, seq, _ = x.shape
    assert seq % QKV_ROW_TILE == 0 and seq % ROW_TILE == 0
    assert seq % KV_CHUNK == 0 and seq % Q_LANES == 0

    t = np.arange(seq)
    row_pos, col_pos = t // GRID_W, t % GRID_W
    tables_a = _rope_tables([t], A_ROT // 2, ROPE_THETA_1D, A_HEAD_DIM, seq)
    tables_b = _rope_tables([row_pos, col_pos], B_HEAD_DIM // 4, ROPE_THETA_AXIAL,
                            B_HEAD_DIM, seq)
    cos_a, sin_a = _rope_angles(t, A_ROT // 2, ROPE_THETA_1D)
    tables_a_t = (_table(cos_a.T), _table(sin_a.T))
    cos_r, sin_r = _rope_angles(row_pos, B_HEAD_DIM // 4, ROPE_THETA_AXIAL)
    cos_c, sin_c = _rope_angles(col_pos, B_HEAD_DIM // 4, ROPE_THETA_AXIAL)
    q_scale_b = B_HEAD_DIM ** -0.5 * LOG2E
    tables_b_t = (_table(q_scale_b * np.concatenate([cos_r, cos_r, cos_c, cos_c], axis=1).T),
                  _table(q_scale_b * np.concatenate([-sin_r, sin_r, -sin_c, sin_c], axis=1).T))

    mod = _mod_call(c, ada_w, ada_b).reshape(DEPTH * batch, 1, 6 * D_MODEL)

    for i in range(DEPTH):
        j = i // 2
        if i % 2 == 0:
            w = a_w_qkv[j].astype(BF16)
            w_qv_t = jnp.concatenate([w[:, :D_MODEL], w[:, 2 * D_MODEL:]], axis=1).T
            q_t, k, v_t = _qkv_call(_qkv_a_kernel, "norm_qkv_rope_a", i, x, mod, norm1_g[i],
                                    w_qv_t, w[:, D_MODEL:2 * D_MODEL], [], tables_a, tables_a_t)
            o = _attn_a_call(i, q_t, k, v_t,
                             (a_lam_q1[j], a_lam_k1[j], a_lam_q2[j], a_lam_k2[j]), a_subln_g[j])
            w_o = a_w_o
        else:
            kv_w = B_N_KV * B_HEAD_DIM
            w = b_w_qkv[j].astype(BF16)
            w_qv_t = jnp.concatenate([w[:, :D_MODEL], w[:, D_MODEL + kv_w:]], axis=1).T
            q_gain = jnp.broadcast_to(b_qnorm_g[j][:, None], (B_HEAD_DIM, QKV_ROW_TILE))
            q_t, k, v_t = _qkv_call(_qkv_b_kernel, "norm_qkv_rope_b", i, x, mod, norm1_g[i],
                                    w_qv_t, w[:, D_MODEL:D_MODEL + kv_w],
                                    [q_gain, b_knorm_g[j].reshape(1, LANES)],
                                    tables_b, tables_b_t)
            o = _attn_b_call(q_t, k, v_t)
            w_o = b_w_o
        x = _post_call(i, i == DEPTH - 1, x, o, mod, w_o, j, norm2_g[i], f_w_in, f_w_out,
                       final_g)
    return x
```

```python
import functools
import math

import jax
import jax.numpy as jnp
import numpy as np
from jax.experimental import pallas as pl
from jax.experimental.pallas import tpu as pltpu

D_MODEL = 1024
DEPTH = 2
GRID_W = 64
EPS = 1e-6
LOG2E = math.log2(math.e)

A_HEAD_DIM = 64
A_N_HEADS = D_MODEL // (2 * A_HEAD_DIM)
A_ROT = A_HEAD_DIM // 4
ROPE_THETA_1D = 500000.0

B_HEAD_DIM = 128
B_N_HEADS = D_MODEL // B_HEAD_DIM
B_N_KV = max(1, B_N_HEADS // 4)
B_GROUP = B_N_HEADS // B_N_KV
ROPE_THETA_AXIAL = 10000.0

D_FF = -(-8 * D_MODEL // (3 * 256)) * 256

LANES = 128
SUBLANES = 8
VMEM_LIMIT_BYTES = 56 * 1024 * 1024

ROW_TILE = 512
QKV_ROW_TILE = 512
PROJ_WIDTH = 1024
Q_LANES = 1024
Q_TILE_A = Q_LANES // 2
Q_TILE_B = Q_LANES // B_GROUP
KV_CHUNK = 256
FF_CHUNK = 256
ONES_ROWS = 16
V_ROWS = LANES + ONES_ROWS
PIPELINE_LAG = 2

BF16 = jnp.bfloat16
F32 = jnp.float32


def _lambda_init(layer_idx):
    return 0.8 - 0.6 * math.exp(-0.3 * layer_idx)


def _rms_scale(x):
    return jax.lax.rsqrt(jnp.mean(x * x, axis=-1, keepdims=True) + EPS)


def _mod_kernel(c_ref, w_ref, b_ref, o_ref):
    c = c_ref[...]
    cond = c * jax.nn.sigmoid(c)
    y = jnp.dot(cond.astype(BF16), w_ref[0].astype(BF16), preferred_element_type=F32)
    o_ref[0] = y + b_ref[0]


def _mod_call(c, ada_w, ada_b):
    rows = c.shape[0]
    width = 3 * D_MODEL
    return pl.pallas_call(
        _mod_kernel,
        grid=(DEPTH, 6 * D_MODEL // width),
        in_specs=[
            pl.BlockSpec((rows, D_MODEL), lambda i, j: (0, 0)),
            pl.BlockSpec((1, D_MODEL, width), lambda i, j: (i, 0, j)),
            pl.BlockSpec((1, 1, width), lambda i, j: (i, 0, j)),
        ],
        out_specs=pl.BlockSpec((1, rows, width), lambda i, j: (i, 0, j)),
        out_shape=jax.ShapeDtypeStruct((DEPTH, rows, 6 * D_MODEL), F32),
        compiler_params=pltpu.CompilerParams(
            dimension_semantics=("arbitrary", "arbitrary"),
            vmem_limit_bytes=VMEM_LIMIT_BYTES),
        name="adaln_mod",
    )(c, ada_w, ada_b.reshape(DEPTH, 1, 6 * D_MODEL))


def _rope(x, cos_t, sin_fwd, sin_bwd, shift):
    fwd = pltpu.roll(x, LANES - shift, axis=1)
    bwd = pltpu.roll(x, shift, axis=1)
    return x * cos_t + fwd * sin_fwd + bwd * sin_bwd


def _modulated_norm(x, g, scale, shift):
    return (x * _rms_scale(x)) * (g * (1.0 + scale)) + shift


def _transposed_projection(w_t_ref, h, block):
    n_rows = w_t_ref.shape[0]
    h_t = h.T
    for r in range(0, n_rows, block):
        yield r, jnp.dot(w_t_ref[r:min(r + block, n_rows), :], h_t, preferred_element_type=F32)


def _store_v_heads(vt_ref, first_head, y):
    ones = jnp.ones((ONES_ROWS, y.shape[1]), BF16)
    for u in range(y.shape[0] // LANES):
        base = (first_head + u) * V_ROWS
        vt_ref[0, base:base + LANES, :] = y[u * LANES:(u + 1) * LANES].astype(BF16)
        vt_ref[0, base + LANES:base + V_ROWS, :] = ones


def _qkv_a_kernel(x_ref, g_ref, shift_ref, scale_ref, wqv_t_ref, wk_ref, cos_ref, sf_ref, sb_ref,
                  cos_t_ref, sin_t_ref, qt_ref, k_ref, vt_ref):
    h = _modulated_norm(x_ref[0], g_ref[...], scale_ref[0], shift_ref[0]).astype(BF16)

    half = A_ROT // 2
    cos_t, sf, sb = cos_ref[...], sf_ref[...], sb_ref[...]
    for c in range(0, D_MODEL, PROJ_WIDTH):
        y = jnp.dot(h, wk_ref[:, c:c + PROJ_WIDTH], preferred_element_type=F32)
        for u in range(0, PROJ_WIDTH, LANES):
            k = _rope(y[:, u:u + LANES], cos_t, sf, sb, half)
            k_ref[0, :, c + u:c + u + LANES] = k.astype(BF16)

    cos_rows, sin_rows = cos_t_ref[...], sin_t_ref[...]
    for r, y in _transposed_projection(wqv_t_ref, h, PROJ_WIDTH):
        if r < D_MODEL:
            pieces = []
            for base in range(0, PROJ_WIDTH, A_HEAD_DIM):
                lo, hi = y[base:base + half], y[base + half:base + A_ROT]
                pieces += [lo * cos_rows - hi * sin_rows, hi * cos_rows + lo * sin_rows,
                           y[base + A_ROT:base + A_HEAD_DIM]]
            q_t = jnp.concatenate(pieces, axis=0) * (A_HEAD_DIM ** -0.5 * LOG2E)
            qt_ref[0, r:r + PROJ_WIDTH, :] = q_t.astype(BF16)
        else:
            _store_v_heads(vt_ref, (r - D_MODEL) // LANES, y)


def _qkv_b_kernel(x_ref, g_ref, shift_ref, scale_ref, wqv_t_ref, wk_ref, qg_ref, kg_ref,
                  cos_ref, sf_ref, sb_ref, cos_t_ref, sin_t_ref, qt_ref, k_ref, vt_ref):
    h = _modulated_norm(x_ref[0], g_ref[...], scale_ref[0], shift_ref[0]).astype(BF16)

    quarter = B_HEAD_DIM // 4
    cos_t, sf, sb = cos_ref[...], sf_ref[...], sb_ref[...]
    y = jnp.dot(h, wk_ref[...], preferred_element_type=F32)
    for u in range(0, wk_ref.shape[1], LANES):
        k = y[:, u:u + LANES]
        k = (k * _rms_scale(k)) * kg_ref[...]
        k = _rope(k, cos_t, sf, sb, quarter)
        k_ref[0, :, u:u + LANES] = k.astype(BF16)

    def swap_pairs(a):
        return jnp.concatenate(
            [a[quarter:2 * quarter], a[:quarter], a[3 * quarter:], a[2 * quarter:3 * quarter]],
            axis=0)

    q_gain = qg_ref[...]
    gain_cos = q_gain * cos_t_ref[...]
    gain_sin = swap_pairs(q_gain) * sin_t_ref[...]
    for r, y in _transposed_projection(wqv_t_ref, h, PROJ_WIDTH):
        if r < D_MODEL:
            for base in range(0, PROJ_WIDTH, B_HEAD_DIM):
                q = y[base:base + B_HEAD_DIM]
                q = q * jax.lax.rsqrt(jnp.mean(q * q, axis=0, keepdims=True) + EPS)
                q = q * gain_cos + swap_pairs(q) * gain_sin
                qt_ref[0, r + base:r + base + B_HEAD_DIM, :] = q.astype(BF16)
        else:
            _store_v_heads(vt_ref, (r - D_MODEL) // LANES, y)


def _mod_spec(mod, layer, chunk):
    batch = mod.shape[0] // DEPTH
    return pl.BlockSpec((1, 1, D_MODEL), lambda b, i: (layer * batch + b, 0, chunk))


def _const_spec(shape):
    return pl.BlockSpec(shape, lambda *_: (0,) * len(shape), pipeline_mode=pl.Buffered(1))


def _qkv_call(kernel_fn, name, layer, x, mod, norm_g, w_qv_t, w_k, extra, tables, tables_t):
    batch, seq, _ = x.shape
    n_q = D_MODEL
    n_v = (w_qv_t.shape[0] - n_q) // LANES * V_ROWS
    n_k = w_k.shape[1]
    table_spec = pl.BlockSpec((QKV_ROW_TILE, LANES), lambda b, i: (i, 0))
    table_t_spec = pl.BlockSpec((tables_t[0].shape[0], QKV_ROW_TILE), lambda b, i: (0, i))
    in_specs = [
        pl.BlockSpec((1, QKV_ROW_TILE, D_MODEL), lambda b, i: (b, i, 0)),
        _const_spec((1, D_MODEL)),
        _mod_spec(mod, layer, 0),
        _mod_spec(mod, layer, 1),
        _const_spec(w_qv_t.shape),
        _const_spec(w_k.shape),
    ] + [_const_spec(e.shape) for e in extra] + [table_spec] * 3 + [table_t_spec] * 2
    return pl.pallas_call(
        kernel_fn,
        grid=(batch, seq // QKV_ROW_TILE),
        in_specs=in_specs,
        out_specs=[
            pl.BlockSpec((1, n_q, QKV_ROW_TILE), lambda b, i: (b, 0, i)),
            pl.BlockSpec((1, QKV_ROW_TILE, n_k), lambda b, i: (b, i, 0)),
            pl.BlockSpec((1, n_v, QKV_ROW_TILE), lambda b, i: (b, 0, i)),
        ],
        out_shape=[
            jax.ShapeDtypeStruct((batch, n_q, seq), BF16),
            jax.ShapeDtypeStruct((batch, seq, n_k), BF16),
            jax.ShapeDtypeStruct((batch, n_v, seq), BF16),
        ],
        compiler_params=pltpu.CompilerParams(
            dimension_semantics=("arbitrary", "arbitrary"),
            vmem_limit_bytes=VMEM_LIMIT_BYTES),
        name=name,
    )(x, norm_g.reshape(1, D_MODEL), mod, mod, w_qv_t, w_k, *extra, *tables, *tables_t)


def _scores_and_weighted_sum(q_t, k_ref, vt_ref, write, read, seq):
    s_w_ref, m_w_ref = write
    s_r_ref, m_r_ref = read
    n = s_w_ref.shape[1]
    groups = KV_CHUNK // SUBLANES
    m_prev = m_r_ref[0:1, :] if vt_ref is not None else None
    col_max = None
    acc = None
    for j in range(seq // KV_CHUNK):
        rows = slice(j * KV_CHUNK, (j + 1) * KV_CHUNK)
        if vt_ref is not None:
            p = jnp.exp2(s_r_ref[rows, :] - m_prev).astype(BF16)
            pv = jnp.dot(vt_ref[0, :, rows], p, preferred_element_type=F32)
            acc = pv if acc is None else acc + pv

        if q_t is not None:
            s = jnp.dot(k_ref[0, rows, :], q_t, preferred_element_type=F32)
            s_w_ref[rows, :] = s
            part = jnp.max(s.reshape(groups, SUBLANES, n), axis=0)
            col_max = part if col_max is None else jnp.maximum(col_max, part)
    if q_t is not None:
        m = jnp.max(col_max, axis=0, keepdims=True)
        m_w_ref[...] = jnp.broadcast_to(m, (SUBLANES, n))
    return acc


def _pipelined_step(n_tiles, buffers, step_fn):
    assert n_tiles % 2 == 0 and n_tiles > PIPELINE_LAG
    t = pl.program_id(0)
    even, odd = (buffers[0], buffers[1]), (buffers[1], buffers[0])
    full = jnp.logical_and(t >= PIPELINE_LAG, t < n_tiles)
    variants = [
        (t == 0, even, (True, False, False)),
        (t == 1, odd, (True, True, False)),
        (jnp.logical_and(full, t % 2 == 0), even, (True, True, True)),
        (jnp.logical_and(full, t % 2 == 1), odd, (True, True, True)),
        (t == n_tiles, even, (False, True, True)),
        (t == n_tiles + 1, odd, (False, False, True)),
    ]
    for condition, (write, read), stages in variants:
        pl.when(condition)(functools.partial(step_fn, write, read, *stages))


def _normalised(acc, cols):
    return acc[:LANES, cols] / acc[LANES:LANES + 1, cols]


def _attn_a_kernel(lam_init, n_tiles, q_ref, k_ref, vt_ref, lq1_ref, lk1_ref, lq2_ref, lk2_ref,
                   g_ref, o_ref, *scratch):
    seq = k_ref.shape[1]
    tq = q_ref.shape[2]

    def step(write, read, scores, weighted, finish):
        if finish:
            acc_done = read[2][...]
            lam = (jnp.exp(jnp.sum(lq1_ref[...] * lk1_ref[...], axis=1, keepdims=True))
                   - jnp.exp(jnp.sum(lq2_ref[...] * lk2_ref[...], axis=1, keepdims=True))
                   + lam_init)
            o_t = (_normalised(acc_done, slice(0, tq))
                   - lam * _normalised(acc_done, slice(tq, 2 * tq)))
            o = o_t.T
            o = (o * _rms_scale(o)) * g_ref[...]
            o_ref[0] = (o * (1.0 - lam_init)).astype(BF16)

        q_t = None
        if scores:
            q = q_ref[0]
            row = jax.lax.broadcasted_iota(jnp.int32, q.shape, 0)
            zero = jnp.zeros_like(q)
            q_t = jnp.concatenate([jnp.where(row < A_HEAD_DIM, q, zero),
                                   jnp.where(row >= A_HEAD_DIM, q, zero)], axis=1)
        if scores or weighted:
            acc = _scores_and_weighted_sum(q_t, k_ref, vt_ref if weighted else None,
                                           write[:2], read[:2], seq)
            if weighted:
                write[2][...] = acc

    _pipelined_step(n_tiles, (scratch[:3], scratch[3:]), step)


def _attn_b_kernel(n_tiles, q_ref, k_ref, vt_ref, o_ref, *scratch):
    seq = k_ref.shape[1]
    tq = q_ref.shape[2]

    def step(write, read, scores, weighted, finish):
        if finish:
            acc_done = read[2][...]
            for g in range(B_GROUP):
                o_t = _normalised(acc_done, slice(g * tq, (g + 1) * tq))
                o_ref[0, :, g * LANES:(g + 1) * LANES] = o_t.T.astype(BF16)

        q_t = None
        if scores:
            q = q_ref[0]
            q_t = jnp.concatenate([q[g * LANES:(g + 1) * LANES, :] for g in range(B_GROUP)],
                                  axis=1)
        if scores or weighted:
            acc = _scores_and_weighted_sum(q_t, k_ref, vt_ref if weighted else None,
                                           write[:2], read[:2], seq)
            if weighted:
                write[2][...] = acc

    _pipelined_step(n_tiles, (scratch[:3], scratch[3:]), step)


def _attn_scratch(seq):
    buffer_set = [pltpu.VMEM((seq, Q_LANES), F32),
                  pltpu.VMEM((SUBLANES, Q_LANES), F32),
                  pltpu.VMEM((V_ROWS, Q_LANES), F32)]
    return buffer_set + buffer_set


def _attn_call(kernel_fn, name, q_t, k, v_t, n_heads, q_rows, tile, extra):
    batch, seq, _ = k.shape
    tiles_per_head = seq // tile
    n_tiles = batch * n_heads * tiles_per_head

    def split(t):
        t = jnp.clip(t, 0, n_tiles - 1)
        return (t // (n_heads * tiles_per_head), (t // tiles_per_head) % n_heads,
                t % tiles_per_head)

    def q_map(t):
        b, h, i = split(t)
        return (b, h, i)

    def k_map(t):
        b, h, _ = split(t)
        return (b, 0, h)

    def v_map(t):
        b, h, _ = split(t - 1)
        return (b, h, 0)

    def o_map(t):
        b, h, i = split(t - PIPELINE_LAG)
        return (b, i, h)

    return pl.pallas_call(
        functools.partial(kernel_fn, n_tiles),
        grid=(n_tiles + PIPELINE_LAG,),
        in_specs=[pl.BlockSpec((1, q_rows, tile), q_map),
                  pl.BlockSpec((1, seq, LANES), k_map),
                  pl.BlockSpec((1, V_ROWS, seq), v_map)] + [_const_spec(e.shape) for e in extra],
        out_specs=pl.BlockSpec((1, tile, D_MODEL // n_heads), o_map),
        out_shape=jax.ShapeDtypeStruct((batch, seq, D_MODEL), BF16),
        scratch_shapes=_attn_scratch(seq),
        compiler_params=pltpu.CompilerParams(
            dimension_semantics=("arbitrary",),
            vmem_limit_bytes=VMEM_LIMIT_BYTES),
        name=name,
    )(q_t, k, v_t, *extra)


def _attn_a_call(layer, q_t, k, v_t, lam_params, subln_g):
    extra = [p.reshape(1, A_HEAD_DIM) for p in lam_params] + [subln_g.reshape(1, LANES)]
    return _attn_call(functools.partial(_attn_a_kernel, _lambda_init(layer)), "diff_attention",
                      q_t, k, v_t, A_N_HEADS, LANES, Q_TILE_A, extra)


def _attn_b_call(q_t, k, v_t):
    return _attn_call(_attn_b_kernel, "axial_gqa_attention", q_t, k, v_t, B_N_KV,
                      B_GROUP * B_HEAD_DIM, Q_TILE_B, [])


def _post_kernel(final_norm, layer, w_o_layer, x_ref, o_ref, wo_hbm, gate1_ref, g2_ref,
                 shift2_ref, scale2_ref, gate2_ref, win_hbm, wout_hbm, fg_ref, out_ref,
                 act_ref, wo_ref, win_ref, wout_ref, sems):
    n_chunks = D_FF // FF_CHUNK

    def cols(c, up):
        start = (D_FF if up else 0) + c * FF_CHUNK
        return slice(start, start + FF_CHUNK)

    pieces = [(wo_hbm.at[w_o_layer], wo_ref)]
    for c in range(n_chunks):
        for up in (False, True):
            pieces.append((win_hbm.at[layer, :, cols(c, up)], win_ref.at[:, cols(c, up)]))
    pieces.append((wout_hbm.at[layer], wout_ref))
    copies = [pltpu.make_async_copy(src, dst, sems.at[i]) for i, (src, dst) in enumerate(pieces)]

    def step(arrived):
        arrived(0)
        y = jnp.dot(o_ref[0], wo_ref[...].astype(BF16), preferred_element_type=F32)
        x1 = x_ref[0] + gate1_ref[0] * y
        out_ref[0] = x1
        h = _modulated_norm(x1, g2_ref[...], scale2_ref[0], shift2_ref[0]).astype(BF16)
        for c in range(n_chunks):
            arrived(1 + 2 * c)
            arrived(2 + 2 * c)
            gate = jnp.dot(h, win_ref[:, cols(c, False)].astype(BF16), preferred_element_type=F32)
            up = jnp.dot(h, win_ref[:, cols(c, True)].astype(BF16), preferred_element_type=F32)
            act_ref[:, cols(c, False)] = ((gate * jax.nn.sigmoid(gate)) * up).astype(BF16)
        arrived(len(copies) - 1)
        f = jnp.dot(act_ref[...], wout_ref[...].astype(BF16), preferred_element_type=F32)
        x2 = out_ref[0] + gate2_ref[0] * f
        if final_norm:
            x2 = (x2 * _rms_scale(x2)) * fg_ref[...]
        out_ref[0] = x2

    first = jnp.logical_and(pl.program_id(0) == 0, pl.program_id(1) == 0)

    @pl.when(first)
    def _():
        for copy in copies:
            copy.start()
        step(lambda i: copies[i].wait())

    @pl.when(jnp.logical_not(first))
    def _():
        step(lambda i: None)


def _post_call(layer, final_norm, x, o, mod, w_o, w_o_layer, norm2_g, w_in, w_out, final_g):
    batch, seq, _ = x.shape
    row_spec = pl.BlockSpec((1, ROW_TILE, D_MODEL), lambda b, i: (b, i, 0))
    hbm_spec = pl.BlockSpec(memory_space=pl.ANY)
    return pl.pallas_call(
        functools.partial(_post_kernel, final_norm, layer, w_o_layer),
        grid=(batch, seq // ROW_TILE),
        in_specs=[
            row_spec,
            row_spec,
            hbm_spec,
            _mod_spec(mod, layer, 2),
            _const_spec((1, D_MODEL)),
            _mod_spec(mod, layer, 3),
            _mod_spec(mod, layer, 4),
            _mod_spec(mod, layer, 5),
            hbm_spec,
            hbm_spec,
            _const_spec((1, D_MODEL)),
        ],
        out_specs=row_spec,
        out_shape=jax.ShapeDtypeStruct((batch, seq, D_MODEL), F32),
        scratch_shapes=[pltpu.VMEM((ROW_TILE, D_FF), BF16),
                        pltpu.VMEM(w_o.shape[1:], w_o.dtype),
                        pltpu.VMEM(w_in.shape[1:], w_in.dtype),
                        pltpu.VMEM(w_out.shape[1:], w_out.dtype),
                        pltpu.SemaphoreType.DMA((2 * (D_FF // FF_CHUNK) + 2,))],
        compiler_params=pltpu.CompilerParams(
            dimension_semantics=("arbitrary", "arbitrary"),
            vmem_limit_bytes=VMEM_LIMIT_BYTES),
        name="wo_ffn",
    )(x, o, w_o, mod, norm2_g.reshape(1, D_MODEL), mod, mod, mod, w_in, w_out,
      final_g.reshape(1, D_MODEL))


def _rope_angles(pos, half, theta):
    freqs = float(theta) ** (-np.arange(half, dtype=np.float64) / half)
    ang = np.asarray(pos, np.float64)[:, None] * freqs[None, :]
    return np.cos(ang), np.sin(ang)


def _rope_tables(pos_per_lane_group, half, theta, period, seq):
    cos_cols, sf_cols, sb_cols = [], [], []
    for pos in pos_per_lane_group:
        c, s = _rope_angles(pos, half, theta)
        z = np.zeros_like(s)
        cos_cols += [c, c]
        sf_cols += [-s, z]
        sb_cols += [z, s]
    rest = period - 2 * half * len(pos_per_lane_group)
    if rest:
        cos_cols.append(np.ones((seq, rest)))
        sf_cols.append(np.zeros((seq, rest)))
        sb_cols.append(np.zeros((seq, rest)))
    reps = LANES // period
    return tuple(_table(np.tile(np.concatenate(cols, axis=1), (1, reps)))
                 for cols in (cos_cols, sf_cols, sb_cols))


def _table(values):
    return jnp.asarray(np.ascontiguousarray(values, dtype=np.float32))


def kernel(x, c, ada_w, ada_b, norm1_g, norm2_g, a_w_qkv, a_w_o, a_lam_q1, a_lam_k1, a_lam_q2,
           a_lam_k2, a_subln_g, b_w_qkv, b_w_o, b_qnorm_g, b_knorm_g, f_w_in, f_w_out, final_g):
    batch, seq, _ = x.shape
    assert seq % QKV_ROW_TILE == 0 and seq % ROW_TILE == 0
    assert seq % KV_CHUNK == 0 and seq % Q_LANES == 0

    t = np.arange(seq)
    row_pos, col_pos = t // GRID_W, t % GRID_W
    tables_a = _rope_tables([t], A_ROT // 2, ROPE_THETA_1D, A_HEAD_DIM, seq)
    tables_b = _rope_tables([row_pos, col_pos], B_HEAD_DIM // 4, ROPE_THETA_AXIAL,
                            B_HEAD_DIM, seq)
    cos_a, sin_a = _rope_angles(t, A_ROT // 2, ROPE_THETA_1D)
    tables_a_t = (_table(cos_a.T), _table(sin_a.T))
    cos_r, sin_r = _rope_angles(row_pos, B_HEAD_DIM // 4, ROPE_THETA_AXIAL)
    cos_c, sin_c = _rope_angles(col_pos, B_HEAD_DIM // 4, ROPE_THETA_AXIAL)
    q_scale_b = B_HEAD_DIM ** -0.5 * LOG2E
    tables_b_t = (_table(q_scale_b * np.concatenate([cos_r, cos_r, cos_c, cos_c], axis=1).T),
                  _table(q_scale_b * np.concatenate([-sin_r, sin_r, -sin_c, sin_c], axis=1).T))

    mod = _mod_call(c, ada_w, ada_b).reshape(DEPTH * batch, 1, 6 * D_MODEL)

    for i in range(DEPTH):
        j = i // 2
        if i % 2 == 0:
            w = a_w_qkv[j].astype(BF16)
            w_qv_t = jnp.concatenate([w[:, :D_MODEL], w[:, 2 * D_MODEL:]], axis=1).T
            q_t, k, v_t = _qkv_call(_qkv_a_kernel, "norm_qkv_rope_a", i, x, mod, norm1_g[i],
                                    w_qv_t, w[:, D_MODEL:2 * D_MODEL], [], tables_a, tables_a_t)
            o = _attn_a_call(i, q_t, k, v_t,
                             (a_lam_q1[j], a_lam_k1[j], a_lam_q2[j], a_lam_k2[j]), a_subln_g[j])
            w_o = a_w_o
        else:
            kv_w = B_N_KV * B_HEAD_DIM
            w = b_w_qkv[j].astype(BF16)
            w_qv_t = jnp.concatenate([w[:, :D_MODEL], w[:, D_MODEL + kv_w:]], axis=1).T
            q_gain = jnp.broadcast_to(b_qnorm_g[j][:, None], (B_HEAD_DIM, QKV_ROW_TILE))
            q_t, k, v_t = _qkv_call(_qkv_b_kernel, "norm_qkv_rope_b", i, x, mod, norm1_g[i],
                                    w_qv_t, w[:, D_MODEL:D_MODEL + kv_w],
                                    [q_gain, b_knorm_g[j].reshape(1, LANES)],
                                    tables_b, tables_b_t)
            o = _attn_b_call(q_t, k, v_t)
            w_o = b_w_o
        x = _post_call(i, i == DEPTH - 1, x, o, mod, w_o, j, norm2_g[i], f_w_in, f_w_out,
                       final_g)
    return x
```
